```python
import jax, jax.numpy as jnp
from jax import lax
import numpy as np

D_MODEL = 1024
BATCH = 4
SEQ = 8192
DEPTH = 1

MIX_WIDTH = D_MODEL
CONV_WIDTH = MIX_WIDTH // 2
CONV_GROUPS = 8
CONV_K = 3
HGRN_WIDTH = MIX_WIDTH - CONV_WIDTH
HGRN_HEAD_DIM = 128
HGRN_HEADS = HGRN_WIDTH // HGRN_HEAD_DIM
CHUNK = 64
IN_COLS = 3 * CONV_WIDTH + 4 * HGRN_WIDTH
N_MEM = 256
MEM_HEADS = 4
MEM_HEAD_DIM = D_MODEL // MEM_HEADS
PEER_HEADS = 8
PEER_N_KEYS = 128
PEER_N_EXPERTS = PEER_N_KEYS * PEER_N_KEYS
PEER_DK = 128
PEER_TOPK = 16
PEER_BLOCK = 128
EPS = 1e-6

kernel_name = "hymba_conv_hgrn2_memxattn_peer"


def rmsnorm(x, w):
    xf = x.astype(jnp.float32)
    y = xf * lax.rsqrt(jnp.mean(xf * xf, axis=-1, keepdims=True) + EPS)
    return (y * w.astype(jnp.float32)).astype(x.dtype)


def short_conv_mixer(b_gate, c_gate, v, conv_w):
    T = v.shape[1]
    u = c_gate * v
    up = jnp.pad(u, ((0, 0), (CONV_K - 1, 0), (0, 0)))
    y = up[:, 0:T] * conv_w[0]
    for j in range(1, CONV_K):
        y = y + up[:, j:j + T] * conv_w[j]
    return b_gate * y


def hgrn2_mixer(q, f, i, g, lb, norm_w):
    B, T, _ = q.shape
    H, Dh, C = HGRN_HEADS, HGRN_HEAD_DIM, CHUNK
    N = T // C
    f32 = jnp.float32
    qa = jax.nn.silu(q.astype(f32))
    fg = lb.astype(f32) + (1.0 - lb.astype(f32)) * jax.nn.sigmoid(f.astype(f32))
    k = 1.0 - fg
    logf = jnp.log(fg)

    def heads(a):
        return a.reshape(B, N, C, H, Dh).transpose(0, 3, 1, 2, 4)

    qh, kh, lh, vh = heads(qa), heads(k), heads(logf), heads(i.astype(f32))
    G = jnp.cumsum(lh, axis=3)
    G_last = G[:, :, :, -1:]
    q_dec = qh * jnp.exp(G)
    k_inv = kh * jnp.exp(-G)
    k_to_end = kh * jnp.exp(G_last - G)
    causal = jnp.tril(jnp.ones((C, C), dtype=bool))
    A = jnp.where(causal, jnp.einsum('bhnck,bhnsk->bhncs', q_dec, k_inv), 0.0)
    o_intra = jnp.einsum('bhncs,bhnsv->bhncv', A, vh)
    U = jnp.einsum('bhnck,bhncv->bhnkv', k_to_end, vh)
    decay = jnp.exp(G_last[:, :, :, 0])

    def step(S, xs):
        d_n, U_n = xs
        return d_n[..., None] * S + U_n, S

    S0 = jnp.zeros((B, H, Dh, Dh), f32)
    _, S_start = lax.scan(step, S0, (jnp.moveaxis(decay, 2, 0), jnp.moveaxis(U, 2, 0)))
    S_start = jnp.moveaxis(S_start, 0, 2)
    o = o_intra + jnp.einsum('bhnck,bhnkv->bhncv', q_dec, S_start)
    o = o.transpose(0, 2, 3, 1, 4).reshape(B, T, H, Dh)
    o = o * lax.rsqrt(jnp.mean(o * o, axis=-1, keepdims=True) + EPS)
    o = o.reshape(B, T, H * Dh) * norm_w.astype(f32) * jax.nn.silu(g.astype(f32))
    return o.astype(q.dtype)


def memory_cross_attention(h, mem_n, wq, wkv, wo):
    B, T, _ = h.shape
    M = mem_n.shape[1]
    q = (h @ wq).reshape(B, T, MEM_HEADS, MEM_HEAD_DIM)
    kv = mem_n @ wkv
    k = kv[..., :D_MODEL].reshape(B, M, MEM_HEADS, MEM_HEAD_DIM)
    v = kv[..., D_MODEL:].reshape(B, M, MEM_HEADS, MEM_HEAD_DIM)
    s = jnp.einsum('bthd,bmhd->bhtm', q, k).astype(jnp.float32) * (MEM_HEAD_DIM ** -0.5)
    p = jax.nn.softmax(s, axis=-1).astype(v.dtype)
    o = jnp.einsum('bhtm,bmhd->bthd', p, v).reshape(B, T, D_MODEL)
    return o @ wo


def peer_ffn(h, w_query, sub_keys, expert_down, expert_up):
    B, T, D = h.shape
    Ntok = B * T
    K = PEER_TOPK
    tokens = h.reshape(Ntok, D)
    q = (tokens @ w_query).reshape(Ntok, PEER_HEADS, 2, PEER_DK)
    s = jnp.einsum('nhpd,hpkd->nhpk', q, sub_keys).astype(jnp.float32)
    s_top, i_top = lax.top_k(s, K)
    cand = (s_top[:, :, 0, :, None] + s_top[:, :, 1, None, :]).reshape(Ntok, PEER_HEADS, K * K)
    cand_idx = (i_top[:, :, 0, :, None] * PEER_N_KEYS + i_top[:, :, 1, None, :]).reshape(Ntok, PEER_HEADS, K * K)
    best, pos = lax.top_k(cand, K)
    expert_idx = jnp.take_along_axis(cand_idx, pos, axis=-1)
    gate = jax.nn.softmax(best, axis=-1)
    nb = Ntok // PEER_BLOCK

    def block(args):
        xb, idx, gb = args
        u = jnp.take(expert_down, idx, axis=0)
        a = jax.nn.gelu(jnp.einsum('nd,nhkd->nhk', xb, u).astype(jnp.float32), approximate=False)
        w = (gb * a).astype(xb.dtype)
        v = jnp.take(expert_up, idx, axis=0)
        return jnp.einsum('nhk,nhkd->nd', w, v)

    out = lax.map(block, (tokens.reshape(nb, PEER_BLOCK, D),
                          expert_idx.reshape(nb, PEER_BLOCK, PEER_HEADS, K),
                          gate.reshape(nb, PEER_BLOCK, PEER_HEADS, K)))
    return out.reshape(B, T, D)


def setup_inputs(seed: int = 0) -> dict:
    key = jax.random.key(seed)
    ks = jax.random.split(key, 24)
    f32 = jnp.float32
    L, D = DEPTH, D_MODEL

    def nrm(k, shape, scale):
        return jax.random.normal(k, shape, f32) * scale

    def gain(k, shape):
        return 1.0 + 0.02 * jax.random.normal(k, shape, f32)

    return {
        "x": jax.random.normal(ks[0], (BATCH, SEQ, D), f32),
        "mem": jax.random.normal(ks[1], (BATCH, N_MEM, D), f32),
        "mix_norm_w": gain(ks[2], (L, D)),
        "w_in": nrm(ks[3], (L, D, IN_COLS), D ** -0.5),
        "conv_w": nrm(ks[4], (L, CONV_K, CONV_WIDTH), CONV_K ** -0.5),
        "conv_norm_w": gain(ks[5], (L, CONV_WIDTH)),
        "hgrn_lb_logits": 0.1 * jax.random.normal(ks[6], (L + 1, HGRN_WIDTH), f32),
        "hgrn_norm_w": gain(ks[7], (L, HGRN_WIDTH)),
        "w_out": nrm(ks[8], (L, MIX_WIDTH, D), MIX_WIDTH ** -0.5),
        "xattn_norm_w": gain(ks[9], (L, D)),
        "mem_norm_w": gain(ks[10], (L, D)),
        "wq_mem": nrm(ks[11], (L, D, D), D ** -0.5),
        "wkv_mem": nrm(ks[12], (L, D, 2 * D), D ** -0.5),
        "wo_mem": nrm(ks[13], (L, D, D), D ** -0.5),
        "ffn_norm_w": gain(ks[14], (L, D)),
        "peer_w_query": nrm(ks[15], (L, D, PEER_HEADS * 2 * PEER_DK), D ** -0.5),
        "peer_sub_keys": nrm(ks[16], (L, PEER_HEADS, 2, PEER_N_KEYS, PEER_DK), PEER_DK ** -0.5),
        "peer_down": nrm(ks[17], (L, PEER_N_EXPERTS, D), D ** -0.5),
        "peer_up": nrm(ks[18], (L, PEER_N_EXPERTS, D), 0.5),
        "final_norm_w": gain(ks[19], (D,)),
    }


def reference(x, mem, mix_norm_w, w_in, conv_w, conv_norm_w, hgrn_lb_logits, hgrn_norm_w, w_out,
              xattn_norm_w, mem_norm_w, wq_mem, wkv_mem, wo_mem, ffn_norm_w,
              peer_w_query, peer_sub_keys, peer_down, peer_up, final_norm_w):
    lb_all = jnp.cumsum(jax.nn.softmax(hgrn_lb_logits.astype(jnp.float32), axis=0), axis=0)
    cw, hw = CONV_WIDTH, HGRN_WIDTH
    for l in range(DEPTH):
        h = rmsnorm(x, mix_norm_w[l])
        proj = h @ w_in[l]
        b_gate = proj[..., 0:cw]
        c_gate = proj[..., cw:2 * cw]
        v_conv = proj[..., 2 * cw:3 * cw]
        o0 = 3 * cw
        q_h = proj[..., o0:o0 + hw]
        f_h = proj[..., o0 + hw:o0 + 2 * hw]
        i_h = proj[..., o0 + 2 * hw:o0 + 3 * hw]
        g_h = proj[..., o0 + 3 * hw:o0 + 4 * hw]
        y_conv = rmsnorm(short_conv_mixer(b_gate, c_gate, v_conv, conv_w[l]), conv_norm_w[l])
        y_hgrn = hgrn2_mixer(q_h, f_h, i_h, g_h, lb_all[l], hgrn_norm_w[l])
        x = x + jnp.concatenate([y_conv, y_hgrn], axis=-1) @ w_out[l]
        h = rmsnorm(x, xattn_norm_w[l])
        mem_n = rmsnorm(mem, mem_norm_w[l])
        x = x + memory_cross_attention(h, mem_n, wq_mem[l], wkv_mem[l], wo_mem[l])
        h = rmsnorm(x, ffn_norm_w[l])
        x = x + peer_ffn(h, peer_w_query[l], peer_sub_keys[l], peer_down[l], peer_up[l])
    return rmsnorm(x, final_norm_w)
```

```python
import functools

import jax
import jax.numpy as jnp
from jax import lax
from jax.experimental import pallas as pl
from jax.experimental.pallas import tpu as pltpu

EPS = 1e-6
CONV_WIDTH = 512
HGRN_HEADS = 4
HGRN_HEAD_DIM = 128
HGRN_WIDTH = HGRN_HEADS * HGRN_HEAD_DIM
CHUNK = 64
MEM_HEADS = 4
PEER_HEADS = 8
PEER_N_KEYS = 128
PEER_DK = 128
PEER_TOPK = 16
V7X_LANES = 128
V7X_VMEM_LIMIT = 56 * 1024 * 1024

BF16 = jnp.bfloat16
F32 = jnp.float32
NT_DIMS = (((1,), (1,)), ((), ()))
TN_DIMS = (((0,), (0,)), ((), ()))


def _rms(x, w):
    return x * lax.rsqrt(jnp.mean(x * x, axis=-1, keepdims=True) + EPS) * w


def _sigmoid(x):
    return 1.0 / (1.0 + jnp.exp(-x))


def _mixer_kernel(x_ref, nw_ref, win_ref, convw_ref, cnw_ref, lbl_ref, hnw_ref, wout_ref, o_ref,
                  proj_ref, ohg_ref, st_ref, halo_ref, *, tm):
    cw, hw, dh = CONV_WIDTH, HGRN_WIDTH, HGRN_HEAD_DIM

    @pl.when(pl.program_id(1) == 0)
    def _():
        st_ref[...] = jnp.zeros_like(st_ref)
        halo_ref[...] = jnp.zeros_like(halo_ref)

    x = x_ref[0]
    h = _rms(x, nw_ref[...]).astype(BF16)
    proj_ref[...] = jnp.dot(h, win_ref[...], preferred_element_type=F32)

    bg = proj_ref[:, 0:cw]
    u = proj_ref[:, cw:2 * cw] * proj_ref[:, 2 * cw:3 * cw]
    row = lax.broadcasted_iota(jnp.int32, (tm, cw), 0)
    prev = halo_ref[...]
    u1 = jnp.where(row == 0, prev[7:8, :], pltpu.roll(u, 1, 0))
    u2 = jnp.where(row == 0, prev[6:7, :], jnp.where(row == 1, prev[7:8, :], pltpu.roll(u, 2, 0)))
    halo_ref[...] = u[tm - 8:tm, :]
    cwt = convw_ref[...]
    yc = _rms(bg * (u2 * cwt[0:1, :] + u1 * cwt[1:2, :] + u * cwt[2:3, :]), cnw_ref[...])

    lbl = lbl_ref[...]
    lmax = jnp.max(lbl, axis=0, keepdims=True)
    lexp = jnp.exp(lbl - lmax)
    lb = lexp[0:1, :] / jnp.sum(lexp, axis=0, keepdims=True)
    o0 = 3 * cw
    rowc = lax.broadcasted_iota(jnp.int32, (CHUNK, hw), 0)
    tril = (lax.broadcasted_iota(jnp.int32, (CHUNK, CHUNK), 0)
            >= lax.broadcasted_iota(jnp.int32, (CHUNK, CHUNK), 1))
    for c in range(tm // CHUNK):
        r0 = c * CHUNK
        qc = proj_ref[r0:r0 + CHUNK, o0:o0 + hw]
        fc = proj_ref[r0:r0 + CHUNK, o0 + hw:o0 + 2 * hw]
        ic = proj_ref[r0:r0 + CHUNK, o0 + 2 * hw:o0 + 3 * hw]
        qa = qc * _sigmoid(qc)
        fg = lb + (1.0 - lb) * _sigmoid(fc)
        kk = 1.0 - fg
        g = jnp.log(fg)
        for s in (1, 2, 4, 8, 16, 32):
            g = g + jnp.where(rowc >= s, pltpu.roll(g, s, 0), 0.0)
        gl = g[CHUNK - 1:CHUNK, :]
        q_dec = (qa * jnp.exp(g)).astype(BF16)
        k_inv = (kk * jnp.exp(-g)).astype(BF16)
        k_end = (kk * jnp.exp(gl - g)).astype(BF16)
        dec = jnp.exp(gl)
        vb = ic.astype(BF16)
        for hd in range(HGRN_HEADS):
            cs = slice(hd * dh, (hd + 1) * dh)
            a = lax.dot_general(q_dec[:, cs], k_inv[:, cs], NT_DIMS, preferred_element_type=F32)
            a = jnp.where(tril, a, 0.0).astype(BF16)
            st = st_ref[hd]
            o = (jnp.dot(a, vb[:, cs], preferred_element_type=F32)
                 + lax.dot_general(q_dec[:, cs], st.astype(BF16), NT_DIMS, preferred_element_type=F32))
            ohg_ref[r0:r0 + CHUNK, cs] = o
            ut = lax.dot_general(vb[:, cs], k_end[:, cs], TN_DIMS, preferred_element_type=F32)
            st_ref[hd] = st * dec[:, cs] + ut

    gate = proj_ref[:, o0 + 3 * hw:o0 + 4 * hw]
    gate = gate * _sigmoid(gate)
    hnw = hnw_ref[...]
    acc = x + jnp.dot(yc.astype(BF16), wout_ref[0:cw, :], preferred_element_type=F32)
    for hd in range(HGRN_HEADS):
        cs = slice(hd * dh, (hd + 1) * dh)
        oh = ohg_ref[:, cs]
        oh = oh * lax.rsqrt(jnp.mean(oh * oh, axis=-1, keepdims=True) + EPS)
        yh = oh * hnw[:, cs] * gate[:, cs]
        acc = acc + jnp.dot(yh.astype(BF16), wout_ref[cw + hd * dh:cw + (hd + 1) * dh, :],
                            preferred_element_type=F32)
    o_ref[0] = acc


def _mixer(x, mix_norm_w, w_in, conv_w, conv_norm_w, lb_logits, hgrn_norm_w, w_out, *, tm):
    b, t, d = x.shape
    ncols = w_in.shape[1]
    const = lambda *_: (0, 0)
    return pl.pallas_call(
        functools.partial(_mixer_kernel, tm=tm),
        grid=(b, t // tm),
        in_specs=[
            pl.BlockSpec((1, tm, d), lambda i, j: (i, j, 0)),
            pl.BlockSpec((1, d), const),
            pl.BlockSpec((d, ncols), const),
            pl.BlockSpec(conv_w.shape, const),
            pl.BlockSpec((1, CONV_WIDTH), const),
            pl.BlockSpec(lb_logits.shape, const),
            pl.BlockSpec((1, HGRN_WIDTH), const),
            pl.BlockSpec((d, d), const),
        ],
        out_specs=pl.BlockSpec((1, tm, d), lambda i, j: (i, j, 0)),
        out_shape=jax.ShapeDtypeStruct((b, t, d), F32),
        scratch_shapes=[
            pltpu.VMEM((tm, ncols), F32),
            pltpu.VMEM((tm, HGRN_WIDTH), F32),
            pltpu.VMEM((HGRN_HEADS, HGRN_HEAD_DIM, HGRN_HEAD_DIM), F32),
            pltpu.VMEM((8, CONV_WIDTH), F32),
        ],
        compiler_params=pltpu.CompilerParams(
            dimension_semantics=("arbitrary", "arbitrary"), vmem_limit_bytes=V7X_VMEM_LIMIT),
        name="mixer",
    )(x, mix_norm_w, w_in, conv_w, conv_norm_w, lb_logits, hgrn_norm_w, w_out)


def _kv_kernel(mem_ref, nw_ref, wkv_ref, o_ref):
    mn = _rms(mem_ref[0], nw_ref[...]).astype(BF16)
    o_ref[0] = jnp.dot(mn, wkv_ref[...], preferred_element_type=F32).astype(BF16)


def _kv(mem, mem_norm_w, wkv):
    b, m, d = mem.shape
    return pl.pallas_call(
        _kv_kernel,
        grid=(b,),
        in_specs=[
            pl.BlockSpec((1, m, d), lambda i: (i, 0, 0)),
            pl.BlockSpec((1, d), lambda i: (0, 0)),
            pl.BlockSpec((d, 2 * d), lambda i: (0, 0)),
        ],
        out_specs=pl.BlockSpec((1, m, 2 * d), lambda i: (i, 0, 0)),
        out_shape=jax.ShapeDtypeStruct((b, m, 2 * d), BF16),
        compiler_params=pltpu.CompilerParams(
            dimension_semantics=("arbitrary",), vmem_limit_bytes=V7X_VMEM_LIMIT),
        name="kv",
    )(mem, mem_norm_w, wkv)


def _xattn_kernel(x_ref, nw_ref, wq_ref, kv_ref, wo_ref, o_ref):
    x = x_ref[0]
    d = x.shape[-1]
    hd_dim = d // MEM_HEADS
    h = _rms(x, nw_ref[...]).astype(BF16)
    q = jnp.dot(h, wq_ref[...], preferred_element_type=F32)
    acc = x
    for hd in range(MEM_HEADS):
        cs = slice(hd * hd_dim, (hd + 1) * hd_dim)
        kh = kv_ref[0, :, cs]
        vh = kv_ref[0, :, d + hd * hd_dim:d + (hd + 1) * hd_dim]
        s = lax.dot_general(q[:, cs].astype(BF16), kh, NT_DIMS, preferred_element_type=F32)
        s = s * (hd_dim ** -0.5)
        p = jnp.exp(s - jnp.max(s, axis=-1, keepdims=True))
        p = p / jnp.sum(p, axis=-1, keepdims=True)
        oh = jnp.dot(p.astype(BF16), vh, preferred_element_type=F32)
        acc = acc + jnp.dot(oh.astype(BF16), wo_ref[cs, :], preferred_element_type=F32)
    o_ref[0] = acc


def _xattn(x, xattn_norm_w, wq, kv, wo, *, tm):
    b, t, d = x.shape
    m = kv.shape[1]
    const = lambda *_: (0, 0)
    return pl.pallas_call(
        _xattn_kernel,
        grid=(b, t // tm),
        in_specs=[
            pl.BlockSpec((1, tm, d), lambda i, j: (i, j, 0)),
            pl.BlockSpec((1, d), const),
            pl.BlockSpec((d, d), const),
            pl.BlockSpec((1, m, 2 * d), lambda i, j: (i, 0, 0)),
            pl.BlockSpec((d, d), const),
        ],
        out_specs=pl.BlockSpec((1, tm, d), lambda i, j: (i, j, 0)),
        out_shape=jax.ShapeDtypeStruct((b, t, d), F32),
        compiler_params=pltpu.CompilerParams(
            dimension_semantics=("arbitrary", "arbitrary"), vmem_limit_bytes=V7X_VMEM_LIMIT),
        name="xattn",
    )(x, xattn_norm_w, wq, kv, wo)


def _staircase():
    return [(q, PEER_TOPK // (q + 1)) for q in range(PEER_TOPK)]


N_CAND = sum(n for _, n in _staircase())
N_CAND_PAD = -(-N_CAND // 8) * 8


def _top_values(s, vals_ref):
    work = s
    rank = jnp.full(s.shape, float(PEER_TOPK), F32)
    for r in range(PEER_TOPK):
        m = jnp.max(work, axis=0, keepdims=True)
        eq = work == m
        rank = jnp.where(eq, float(r), rank)
        work = jnp.where(eq, -jnp.inf, work)
        vals_ref[r:r + 1, :] = m
    return rank


def _route_kernel(x_ref, nw_ref, wqry_ref, keys_ref, hb_ref, r2_ref, n1_ref, g1_ref, g2_ref,
                  a_ref, b_ref, cand_ref):
    h = _rms(x_ref[...], nw_ref[...]).astype(BF16)
    hb_ref[...] = h
    q = jnp.dot(h, wqry_ref[...], preferred_element_type=F32).astype(BF16)
    tm = q.shape[0]
    for hd in range(PEER_HEADS):
        c1 = (2 * hd) * PEER_DK
        c2 = (2 * hd + 1) * PEER_DK
        s1 = lax.dot_general(keys_ref[2 * hd], q[:, c1:c1 + PEER_DK], NT_DIMS, preferred_element_type=F32)
        s2 = lax.dot_general(keys_ref[2 * hd + 1], q[:, c2:c2 + PEER_DK], NT_DIMS, preferred_element_type=F32)
        _top_values(s1, a_ref)
        rank2 = _top_values(s2, b_ref)
        off = 0
        for qq, n in _staircase():
            cand_ref[off:off + n, :] = a_ref[0:n, :] + b_ref[qq:qq + 1, :]
            off += n
        if N_CAND_PAD > N_CAND:
            cand_ref[N_CAND:N_CAND_PAD, :] = jnp.full((N_CAND_PAD - N_CAND, tm), -jnp.inf, F32)
        cand = cand_ref[...]
        work = cand
        tau = None
        for _ in range(PEER_TOPK):
            tau = jnp.max(work, axis=0, keepdims=True)
            work = jnp.where(work == tau, -jnp.inf, work)
        cmax = cand[0:1, :]
        z = jnp.sum(jnp.where(cand >= tau, jnp.exp(cand - cmax), 0.0), axis=0, keepdims=True)
        n1 = jnp.zeros(s1.shape, F32)
        for qq in range(PEER_TOPK):
            n1 = n1 + jnp.where(s1 + b_ref[qq:qq + 1, :] >= tau, 1.0, 0.0)
        r2_ref[hd] = rank2
        n1_ref[hd] = n1
        g1_ref[hd] = jnp.exp(s1 - a_ref[0:1, :])
        g2_ref[hd] = jnp.exp(s2 - b_ref[0:1, :]) / z


def _route(x2, ffn_norm_w, w_query, keys, *, tm):
    n, d = x2.shape
    nq = w_query.shape[1]
    tab = jax.ShapeDtypeStruct((PEER_HEADS, PEER_N_KEYS, n), F32)
    tab_spec = pl.BlockSpec((PEER_HEADS, PEER_N_KEYS, tm), lambda i: (0, 0, i))
    return pl.pallas_call(
        _route_kernel,
        grid=(n // tm,),
        in_specs=[
            pl.BlockSpec((tm, d), lambda i: (i, 0)),
            pl.BlockSpec((1, d), lambda i: (0, 0)),
            pl.BlockSpec((d, nq), lambda i: (0, 0)),
            pl.BlockSpec(keys.shape, lambda i: (0, 0, 0)),
        ],
        out_specs=[pl.BlockSpec((tm, d), lambda i: (i, 0)), tab_spec, tab_spec, tab_spec, tab_spec],
        out_shape=[jax.ShapeDtypeStruct((n, d), BF16), tab, tab, tab, tab],
        scratch_shapes=[
            pltpu.VMEM((PEER_TOPK, tm), F32),
            pltpu.VMEM((PEER_TOPK, tm), F32),
            pltpu.VMEM((N_CAND_PAD, tm), F32),
        ],
        compiler_params=pltpu.CompilerParams(
            dimension_semantics=("arbitrary",), vmem_limit_bytes=V7X_VMEM_LIMIT),
        name="route",
    )(x2, ffn_norm_w, w_query, keys)


def _tables_kernel(down_ref, up_ref, u_ref, vt_ref):
    u_ref[...] = down_ref[...].astype(BF16)
    vt_ref[...] = up_ref[...].T.astype(BF16)


def _tables(peer_down, peer_up, *, te):
    ne, d = peer_down.shape
    return pl.pallas_call(
        _tables_kernel,
        grid=(ne // te,),
        in_specs=[pl.BlockSpec((te, d), lambda i: (i, 0)), pl.BlockSpec((te, d), lambda i: (i, 0))],
        out_specs=[pl.BlockSpec((te, d), lambda i: (i, 0)), pl.BlockSpec((d, te), lambda i: (0, i))],
        out_shape=[jax.ShapeDtypeStruct((ne, d), BF16), jax.ShapeDtypeStruct((d, ne), BF16)],
        compiler_params=pltpu.CompilerParams(
            dimension_semantics=("arbitrary",), vmem_limit_bytes=V7X_VMEM_LIMIT),
        name="tables",
    )(peer_down, peer_up)


def _peer_kernel(hb_ref, u_ref, vt_ref, r2_ref, n1_ref, g1_ref, g2_ref, x2_ref, fnw_ref, o_ref,
                 s_ref, h_ref, acc_ref, *, tt, te):
    e = pl.program_id(1)

    @pl.when(e == 0)
    def _():
        acc_ref[...] = jnp.zeros_like(acc_ref)

    s_ref[...] = lax.dot_general(u_ref[...], hb_ref[...], NT_DIMS, preferred_element_type=F32)
    ni = te // PEER_N_KEYS
    irows = pl.ds(pl.multiple_of(e * ni, ni), ni)
    for ii in range(ni):
        rs = slice(ii * PEER_N_KEYS, (ii + 1) * PEER_N_KEYS)
        for lg in range(tt // V7X_LANES):
            ls = slice(lg * V7X_LANES, (lg + 1) * V7X_LANES)
            w = jnp.zeros((PEER_N_KEYS, V7X_LANES), F32)
            for hd in range(PEER_HEADS):
                n1 = n1_ref[hd, irows, ls][ii:ii + 1, :]
                g1 = g1_ref[hd, irows, ls][ii:ii + 1, :]
                w = w + jnp.where(r2_ref[hd, :, ls] < n1, g2_ref[hd, :, ls] * g1, 0.0)
            s = s_ref[rs, ls]
            act = 0.5 * s * (1.0 + lax.erf(s * 0.7071067811865476))
            h_ref[rs, ls] = (w * act).astype(BF16)
    acc_ref[...] += jnp.dot(vt_ref[...], h_ref[...], preferred_element_type=F32)

    @pl.when(e == pl.num_programs(1) - 1)
    def _():
        z = x2_ref[...] + acc_ref[...].T
        o_ref[...] = _rms(z, fnw_ref[...])


def _peer(hb, u, vt, r2, n1, g1, g2, x2, final_norm_w, *, tt, te):
    n, d = hb.shape
    ne = u.shape[0]
    tab_spec = pl.BlockSpec((PEER_HEADS, PEER_N_KEYS, tt), lambda i, j: (0, 0, i))
    return pl.pallas_call(
        functools.partial(_peer_kernel, tt=tt, te=te),
        grid=(n // tt, ne // te),
        in_specs=[
            pl.BlockSpec((tt, d), lambda i, j: (i, 0)),
            pl.BlockSpec((te, d), lambda i, j: (j, 0)),
            pl.BlockSpec((d, te), lambda i, j: (0, j)),
            tab_spec, tab_spec, tab_spec, tab_spec,
            pl.BlockSpec((tt, d), lambda i, j: (i, 0)),
            pl.BlockSpec((1, d), lambda i, j: (0, 0)),
        ],
        out_specs=pl.BlockSpec((tt, d), lambda i, j: (i, 0)),
        out_shape=jax.ShapeDtypeStruct((n, d), F32),
        scratch_shapes=[
            pltpu.VMEM((te, tt), F32),
            pltpu.VMEM((te, tt), BF16),
            pltpu.VMEM((d, tt), F32),
        ],
        compiler_params=pltpu.CompilerParams(
            dimension_semantics=("arbitrary", "arbitrary"), vmem_limit_bytes=V7X_VMEM_LIMIT),
        name="peer",
    )(hb, u, vt, r2, n1, g1, g2, x2, final_norm_w)


def kernel(x, mem, mix_norm_w, w_in, conv_w, conv_norm_w, hgrn_lb_logits, hgrn_norm_w, w_out, xattn_norm_w,
           mem_norm_w, wq_mem, wkv_mem, wo_mem, ffn_norm_w, peer_w_query, peer_sub_keys, peer_down, peer_up,
           final_norm_w):
    b, t, d = x.shape
    depth = w_in.shape[0]
    assert depth == 1, "single-layer problem"
    assert t % 512 == 0 and d % V7X_LANES == 0
    row = lambda v: v.reshape(1, -1)
    x1 = _mixer(x, row(mix_norm_w[0]), w_in[0].astype(BF16), conv_w[0], row(conv_norm_w[0]), hgrn_lb_logits,
                row(hgrn_norm_w[0]), w_out[0].astype(BF16), tm=512)
    kv = _kv(mem, row(mem_norm_w[0]), wkv_mem[0].astype(BF16))
    x2 = _xattn(x1, row(xattn_norm_w[0]), wq_mem[0].astype(BF16), kv, wo_mem[0].astype(BF16), tm=512)
    x2 = x2.reshape(b * t, d)
    keys = peer_sub_keys[0].reshape(2 * PEER_HEADS, PEER_N_KEYS, PEER_DK).astype(BF16)
    hb, r2, n1, g1, g2 = _route(x2, row(ffn_norm_w[0]), peer_w_query[0].astype(BF16), keys, tm=512)
    u, vt = _tables(peer_down[0], peer_up[0], te=512)
    out = _peer(hb, u, vt, r2, n1, g1, g2, x2, row(final_norm_w), tt=512, te=1024)
    return out.reshape(b, t, d)
```

```python
import functools

import jax
import jax.numpy as jnp
from jax import lax
from jax.experimental import pallas as pl
from jax.experimental.pallas import tpu as pltpu

EPS = 1e-6
CONV_WIDTH = 512
HGRN_HEADS = 4
HGRN_HEAD_DIM = 128
HGRN_WIDTH = HGRN_HEADS * HGRN_HEAD_DIM
CHUNK = 64
MEM_HEADS = 4
PEER_HEADS = 8
PEER_N_KEYS = 128
PEER_DK = 128
PEER_TOPK = 16
V7X_LANES = 128
BF16_SUBLANES = 16
V7X_VMEM_LIMIT = 56 * 1024 * 1024

BF16 = jnp.bfloat16
F32 = jnp.float32
NT_DIMS = (((1,), (1,)), ((), ()))
TN_DIMS = (((0,), (0,)), ((), ()))


def _rms(x, w):
    return x * lax.rsqrt(jnp.mean(x * x, axis=-1, keepdims=True) + EPS) * w


def _sigmoid(x):
    return 1.0 / (1.0 + jnp.exp(-x))


def _mixer_kernel(x_ref, nw_ref, win_ref, convw_ref, cnw_ref, lbl_ref, hnw_ref, wout_ref, o_ref,
                  proj_ref, ohg_ref, st_ref, halo_ref, *, tm):
    cw, hw, dh = CONV_WIDTH, HGRN_WIDTH, HGRN_HEAD_DIM

    @pl.when(pl.program_id(1) == 0)
    def _():
        st_ref[...] = jnp.zeros_like(st_ref)
        halo_ref[...] = jnp.zeros_like(halo_ref)

    x = x_ref[0]
    h = _rms(x, nw_ref[...]).astype(BF16)
    proj_ref[...] = jnp.dot(h, win_ref[...], preferred_element_type=F32)

    bg = proj_ref[:, 0:cw]
    u = proj_ref[:, cw:2 * cw] * proj_ref[:, 2 * cw:3 * cw]
    row = lax.broadcasted_iota(jnp.int32, (tm, cw), 0)
    prev = halo_ref[...]
    u1 = jnp.where(row == 0, prev[7:8, :], pltpu.roll(u, 1, 0))
    u2 = jnp.where(row == 0, prev[6:7, :], jnp.where(row == 1, prev[7:8, :], pltpu.roll(u, 2, 0)))
    halo_ref[...] = u[tm - 8:tm, :]
    cwt = convw_ref[...]
    yc = _rms(bg * (u2 * cwt[0:1, :] + u1 * cwt[1:2, :] + u * cwt[2:3, :]), cnw_ref[...])

    lbl = lbl_ref[...]
    lmax = jnp.max(lbl, axis=0, keepdims=True)
    lexp = jnp.exp(lbl - lmax)
    lb = lexp[0:1, :] / jnp.sum(lexp, axis=0, keepdims=True)
    o0 = 3 * cw
    rowc = lax.broadcasted_iota(jnp.int32, (CHUNK, hw), 0)
    tril = (lax.broadcasted_iota(jnp.int32, (CHUNK, CHUNK), 0)
            >= lax.broadcasted_iota(jnp.int32, (CHUNK, CHUNK), 1))
    for c in range(tm // CHUNK):
        r0 = c * CHUNK
        qc = proj_ref[r0:r0 + CHUNK, o0:o0 + hw]
        fc = proj_ref[r0:r0 + CHUNK, o0 + hw:o0 + 2 * hw]
        ic = proj_ref[r0:r0 + CHUNK, o0 + 2 * hw:o0 + 3 * hw]
        qa = qc * _sigmoid(qc)
        fg = lb + (1.0 - lb) * _sigmoid(fc)
        kk = 1.0 - fg
        g = jnp.log(fg)
        for s in (1, 2, 4, 8, 16, 32):
            g = g + jnp.where(rowc >= s, pltpu.roll(g, s, 0), 0.0)
        gl = g[CHUNK - 1:CHUNK, :]
        q_dec = (qa * jnp.exp(g)).astype(BF16)
        k_inv = (kk * jnp.exp(-g)).astype(BF16)
        k_end = (kk * jnp.exp(gl - g)).astype(BF16)
        dec = jnp.exp(gl)
        vb = ic.astype(BF16)
        for hd in range(HGRN_HEADS):
            cs = slice(hd * dh, (hd + 1) * dh)
            a = lax.dot_general(q_dec[:, cs], k_inv[:, cs], NT_DIMS, preferred_element_type=F32)
            a = jnp.where(tril, a, 0.0).astype(BF16)
            st = st_ref[hd]
            o = (jnp.dot(a, vb[:, cs], preferred_element_type=F32)
                 + lax.dot_general(q_dec[:, cs], st.astype(BF16), NT_DIMS, preferred_element_type=F32))
            ohg_ref[r0:r0 + CHUNK, cs] = o
            ut = lax.dot_general(vb[:, cs], k_end[:, cs], TN_DIMS, preferred_element_type=F32)
            st_ref[hd] = st * dec[:, cs] + ut

    gate = proj_ref[:, o0 + 3 * hw:o0 + 4 * hw]
    gate = gate * _sigmoid(gate)
    hnw = hnw_ref[...]
    acc = x + jnp.dot(yc.astype(BF16), wout_ref[0:cw, :], preferred_element_type=F32)
    for hd in range(HGRN_HEADS):
        cs = slice(hd * dh, (hd + 1) * dh)
        oh = ohg_ref[:, cs]
        oh = oh * lax.rsqrt(jnp.mean(oh * oh, axis=-1, keepdims=True) + EPS)
        yh = oh * hnw[:, cs] * gate[:, cs]
        acc = acc + jnp.dot(yh.astype(BF16), wout_ref[cw + hd * dh:cw + (hd + 1) * dh, :],
                            preferred_element_type=F32)
    o_ref[0] = acc


def _mixer(x, mix_norm_w, w_in, conv_w, conv_norm_w, lb_logits, hgrn_norm_w, w_out, *, tm):
    b, t, d = x.shape
    ncols = w_in.shape[1]
    const = lambda *_: (0, 0)
    return pl.pallas_call(
        functools.partial(_mixer_kernel, tm=tm),
        grid=(b, t // tm),
        in_specs=[
            pl.BlockSpec((1, tm, d), lambda i, j: (i, j, 0)),
            pl.BlockSpec((1, d), const),
            pl.BlockSpec((d, ncols), const),
            pl.BlockSpec(conv_w.shape, const),
            pl.BlockSpec((1, CONV_WIDTH), const),
            pl.BlockSpec(lb_logits.shape, const),
            pl.BlockSpec((1, HGRN_WIDTH), const),
            pl.BlockSpec((d, d), const),
        ],
        out_specs=pl.BlockSpec((1, tm, d), lambda i, j: (i, j, 0)),
        out_shape=jax.ShapeDtypeStruct((b, t, d), F32),
        scratch_shapes=[
            pltpu.VMEM((tm, ncols), F32),
            pltpu.VMEM((tm, HGRN_WIDTH), F32),
            pltpu.VMEM((HGRN_HEADS, HGRN_HEAD_DIM, HGRN_HEAD_DIM), F32),
            pltpu.VMEM((8, CONV_WIDTH), F32),
        ],
        compiler_params=pltpu.CompilerParams(
            dimension_semantics=("arbitrary", "arbitrary"), vmem_limit_bytes=V7X_VMEM_LIMIT),
        name="mixer",
    )(x, mix_norm_w, w_in, conv_w, conv_norm_w, lb_logits, hgrn_norm_w, w_out)


def _kv_kernel(mem_ref, nw_ref, wkv_ref, o_ref):
    mn = _rms(mem_ref[0], nw_ref[...]).astype(BF16)
    o_ref[0] = jnp.dot(mn, wkv_ref[...], preferred_element_type=F32).astype(BF16)


def _kv(mem, mem_norm_w, wkv):
    b, m, d = mem.shape
    return pl.pallas_call(
        _kv_kernel,
        grid=(b,),
        in_specs=[
            pl.BlockSpec((1, m, d), lambda i: (i, 0, 0)),
            pl.BlockSpec((1, d), lambda i: (0, 0)),
            pl.BlockSpec((d, 2 * d), lambda i: (0, 0)),
        ],
        out_specs=pl.BlockSpec((1, m, 2 * d), lambda i: (i, 0, 0)),
        out_shape=jax.ShapeDtypeStruct((b, m, 2 * d), BF16),
        compiler_params=pltpu.CompilerParams(
            dimension_semantics=("arbitrary",), vmem_limit_bytes=V7X_VMEM_LIMIT),
        name="kv",
    )(mem, mem_norm_w, wkv)


def _xattn_kernel(x_ref, nw_ref, wq_ref, kv_ref, wo_ref, o_ref):
    x = x_ref[0]
    d = x.shape[-1]
    hd_dim = d // MEM_HEADS
    h = _rms(x, nw_ref[...]).astype(BF16)
    q = jnp.dot(h, wq_ref[...], preferred_element_type=F32)
    acc = x
    for hd in range(MEM_HEADS):
        cs = slice(hd * hd_dim, (hd + 1) * hd_dim)
        kh = kv_ref[0, :, cs]
        vh = kv_ref[0, :, d + hd * hd_dim:d + (hd + 1) * hd_dim]
        s = lax.dot_general(q[:, cs].astype(BF16), kh, NT_DIMS, preferred_element_type=F32)
        s = s * (hd_dim ** -0.5)
        p = jnp.exp(s - jnp.max(s, axis=-1, keepdims=True))
        p = p / jnp.sum(p, axis=-1, keepdims=True)
        oh = jnp.dot(p.astype(BF16), vh, preferred_element_type=F32)
        acc = acc + jnp.dot(oh.astype(BF16), wo_ref[cs, :], preferred_element_type=F32)
    o_ref[0] = acc


def _xattn(x, xattn_norm_w, wq, kv, wo, *, tm):
    b, t, d = x.shape
    m = kv.shape[1]
    const = lambda *_: (0, 0)
    return pl.pallas_call(
        _xattn_kernel,
        grid=(b, t // tm),
        in_specs=[
            pl.BlockSpec((1, tm, d), lambda i, j: (i, j, 0)),
            pl.BlockSpec((1, d), const),
            pl.BlockSpec((d, d), const),
            pl.BlockSpec((1, m, 2 * d), lambda i, j: (i, 0, 0)),
            pl.BlockSpec((d, d), const),
        ],
        out_specs=pl.BlockSpec((1, tm, d), lambda i, j: (i, j, 0)),
        out_shape=jax.ShapeDtypeStruct((b, t, d), F32),
        compiler_params=pltpu.CompilerParams(
            dimension_semantics=("arbitrary", "arbitrary"), vmem_limit_bytes=V7X_VMEM_LIMIT),
        name="xattn",
    )(x, xattn_norm_w, wq, kv, wo)


def _staircase():
    return [(q, PEER_TOPK // (q + 1)) for q in range(PEER_TOPK)]


N_CAND = sum(n for _, n in _staircase())
N_CAND_PAD = -(-N_CAND // 8) * 8


def _top_values(s, vals_ref):
    work = s
    rank = jnp.full(s.shape, float(PEER_TOPK), F32)
    for r in range(PEER_TOPK):
        m = jnp.max(work, axis=0, keepdims=True)
        eq = work == m
        rank = jnp.where(eq, float(r), rank)
        work = jnp.where(eq, -jnp.inf, work)
        vals_ref[r:r + 1, :] = m
    return rank


def _route_kernel(x_ref, nw_ref, wqry_ref, keys_ref, hb_ref, r2_ref, n1_ref, g1_ref, g2_ref,
                  a_ref, b_ref, cand_ref):
    h = _rms(x_ref[...], nw_ref[...]).astype(BF16)
    hb_ref[...] = h
    q = jnp.dot(h, wqry_ref[...], preferred_element_type=F32).astype(BF16)
    tm = q.shape[0]
    for hd in range(PEER_HEADS):
        c1 = (2 * hd) * PEER_DK
        c2 = (2 * hd + 1) * PEER_DK
        s1 = lax.dot_general(keys_ref[2 * hd], q[:, c1:c1 + PEER_DK], NT_DIMS, preferred_element_type=F32)
        s2 = lax.dot_general(keys_ref[2 * hd + 1], q[:, c2:c2 + PEER_DK], NT_DIMS, preferred_element_type=F32)
        _top_values(s1, a_ref)
        rank2 = _top_values(s2, b_ref)
        off = 0
        for qq, n in _staircase():
            cand_ref[off:off + n, :] = a_ref[0:n, :] + b_ref[qq:qq + 1, :]
            off += n
        if N_CAND_PAD > N_CAND:
            cand_ref[N_CAND:N_CAND_PAD, :] = jnp.full((N_CAND_PAD - N_CAND, tm), -jnp.inf, F32)
        cand = cand_ref[...]
        work = cand
        tau = None
        for _ in range(PEER_TOPK):
            tau = jnp.max(work, axis=0, keepdims=True)
            work = jnp.where(work == tau, -jnp.inf, work)
        cmax = cand[0:1, :]
        z = jnp.sum(jnp.where(cand >= tau, jnp.exp(cand - cmax), 0.0), axis=0, keepdims=True)
        n1 = jnp.zeros(s1.shape, F32)
        for qq in range(PEER_TOPK):
            n1 = n1 + jnp.where(s1 + b_ref[qq:qq + 1, :] >= tau, 1.0, 0.0)
        r2_ref[hd] = rank2.astype(BF16)
        n1_ref[hd] = n1
        g1_ref[hd] = jnp.exp(s1 - a_ref[0:1, :])
        g2_ref[hd] = (jnp.exp(s2 - b_ref[0:1, :]) / z).astype(BF16)


def _route(x2, ffn_norm_w, w_query, keys, *, tm):
    n, d = x2.shape
    nq = w_query.shape[1]
    tab = jax.ShapeDtypeStruct((PEER_HEADS, PEER_N_KEYS, n), F32)
    tabh = jax.ShapeDtypeStruct((PEER_HEADS, PEER_N_KEYS, n), BF16)
    tab_spec = pl.BlockSpec((PEER_HEADS, PEER_N_KEYS, tm), lambda i: (0, 0, i))
    return pl.pallas_call(
        _route_kernel,
        grid=(n // tm,),
        in_specs=[
            pl.BlockSpec((tm, d), lambda i: (i, 0)),
            pl.BlockSpec((1, d), lambda i: (0, 0)),
            pl.BlockSpec((d, nq), lambda i: (0, 0)),
            pl.BlockSpec(keys.shape, lambda i: (0, 0, 0)),
        ],
        out_specs=[pl.BlockSpec((tm, d), lambda i: (i, 0)), tab_spec, tab_spec, tab_spec, tab_spec],
        out_shape=[jax.ShapeDtypeStruct((n, d), BF16), tabh, tab, tab, tabh],
        scratch_shapes=[
            pltpu.VMEM((PEER_TOPK, tm), F32),
            pltpu.VMEM((PEER_TOPK, tm), F32),
            pltpu.VMEM((N_CAND_PAD, tm), F32),
        ],
        compiler_params=pltpu.CompilerParams(
            dimension_semantics=("arbitrary",), vmem_limit_bytes=V7X_VMEM_LIMIT),
        name="route",
    )(x2, ffn_norm_w, w_query, keys)


def _tables_kernel(down_ref, up_ref, u_ref, vt_ref):
    u_ref[...] = down_ref[...].astype(BF16)
    vt_ref[...] = up_ref[...].T.astype(BF16)


def _tables(peer_down, peer_up, *, te):
    ne, d = peer_down.shape
    return pl.pallas_call(
        _tables_kernel,
        grid=(ne // te,),
        in_specs=[pl.BlockSpec((te, d), lambda i: (i, 0)), pl.BlockSpec((te, d), lambda i: (i, 0))],
        out_specs=[pl.BlockSpec((te, d), lambda i: (i, 0)), pl.BlockSpec((d, te), lambda i: (0, i))],
        out_shape=[jax.ShapeDtypeStruct((ne, d), BF16), jax.ShapeDtypeStruct((d, ne), BF16)],
        compiler_params=pltpu.CompilerParams(
            dimension_semantics=("arbitrary",), vmem_limit_bytes=V7X_VMEM_LIMIT),
        name="tables",
    )(peer_down, peer_up)


def _peer_kernel(hb_ref, u_ref, vt_ref, r2_ref, n1_ref, g1_ref, g2_ref, x2_ref, fnw_ref, o_ref,
                 s_ref, h_ref, acc_ref, *, tt, te, sub):
    e = pl.program_id(1)

    @pl.when(e == 0)
    def _():
        acc_ref[...] = jnp.zeros_like(acc_ref)

    nk = PEER_N_KEYS
    lw = 2 * V7X_LANES
    ni = te // nk
    irows = pl.ds(pl.multiple_of(e * ni, ni), ni)
    hb = hb_ref[...]

    def pre_activations(c):
        cs = slice(c * sub, (c + 1) * sub)
        s_ref[cs, :] = lax.dot_general(u_ref[cs, :], hb, NT_DIMS, preferred_element_type=F32)

    nsub = te // sub
    pre_activations(0)
    for c in range(nsub):
        cs = slice(c * sub, (c + 1) * sub)
        if c + 1 < nsub:
            pre_activations(c + 1)
        for ii in range(c * sub // nk, (c + 1) * sub // nk):
            rs = slice(ii * nk, (ii + 1) * nk)
            for lg in range(tt // lw):
                ls = slice(lg * lw, (lg + 1) * lw)
                w = jnp.zeros((nk, lw), BF16)
                for hd in range(PEER_HEADS):
                    n1 = n1_ref[hd, irows, ls][ii:ii + 1, :].astype(BF16)
                    g1 = g1_ref[hd, irows, ls][ii:ii + 1, :].astype(BF16)
                    g2 = g2_ref[hd, :, ls]
                    w = w + jnp.where(r2_ref[hd, :, ls] < n1, g2 * g1, jnp.zeros_like(g2))
                s = s_ref[rs, ls]
                act = 0.5 * s * (1.0 + lax.erf(s * 0.7071067811865476))
                h_ref[rs, ls] = w * act.astype(BF16)
        acc_ref[...] += jnp.dot(vt_ref[:, cs], h_ref[cs, :], preferred_element_type=F32)

    @pl.when(e == pl.num_programs(1) - 1)
    def _():
        z = x2_ref[...] + acc_ref[...].T
        o_ref[...] = _rms(z, fnw_ref[...])


def _peer(hb, u, vt, r2, n1, g1, g2, x2, final_norm_w, *, tt, te, sub):
    n, d = hb.shape
    ne = u.shape[0]
    assert (te // PEER_N_KEYS) % BF16_SUBLANES == 0 and te % sub == 0 and sub % PEER_N_KEYS == 0
    tab_spec = pl.BlockSpec((PEER_HEADS, PEER_N_KEYS, tt), lambda i, j: (0, 0, i))
    return pl.pallas_call(
        functools.partial(_peer_kernel, tt=tt, te=te, sub=sub),
        grid=(n // tt, ne // te),
        in_specs=[
            pl.BlockSpec((tt, d), lambda i, j: (i, 0)),
            pl.BlockSpec((te, d), lambda i, j: (j, 0)),
            pl.BlockSpec((d, te), lambda i, j: (0, j)),
            tab_spec, tab_spec, tab_spec, tab_spec,
            pl.BlockSpec((tt, d), lambda i, j: (i, 0)),
            pl.BlockSpec((1, d), lambda i, j: (0, 0)),
        ],
        out_specs=pl.BlockSpec((tt, d), lambda i, j: (i, 0)),
        out_shape=jax.ShapeDtypeStruct((n, d), F32),
        scratch_shapes=[
            pltpu.VMEM((te, tt), F32),
            pltpu.VMEM((te, tt), BF16),
            pltpu.VMEM((d, tt), F32),
        ],
        compiler_params=pltpu.CompilerParams(
            dimension_semantics=("arbitrary", "arbitrary"), vmem_limit_bytes=V7X_VMEM_LIMIT),
        name="peer",
    )(hb, u, vt, r2, n1, g1, g2, x2, final_norm_w)


def kernel(x, mem, mix_norm_w, w_in, conv_w, conv_norm_w, hgrn_lb_logits, hgrn_norm_w, w_out, xattn_norm_w,
           mem_norm_w, wq_mem, wkv_mem, wo_mem, ffn_norm_w, peer_w_query, peer_sub_keys, peer_down, peer_up,
           final_norm_w):
    b, t, d = x.shape
    depth = w_in.shape[0]
    assert depth == 1, "single-layer problem"
    assert t % 512 == 0 and d % V7X_LANES == 0
    row = lambda v: v.reshape(1, -1)
    x1 = _mixer(x, row(mix_norm_w[0]), w_in[0].astype(BF16), conv_w[0], row(conv_norm_w[0]), hgrn_lb_logits,
                row(hgrn_norm_w[0]), w_out[0].astype(BF16), tm=512)
    kv = _kv(mem, row(mem_norm_w[0]), wkv_mem[0].astype(BF16))
    x2 = _xattn(x1, row(xattn_norm_w[0]), wq_mem[0].astype(BF16), kv, wo_mem[0].astype(BF16), tm=512)
    x2 = x2.reshape(b * t, d)
    keys = peer_sub_keys[0].reshape(2 * PEER_HEADS, PEER_N_KEYS, PEER_DK).astype(BF16)
    hb, r2, n1, g1, g2 = _route(x2, row(ffn_norm_w[0]), peer_w_query[0].astype(BF16), keys, tm=512)
    u, vt = _tables(peer_down[0], peer_up[0], te=512)
    out = _peer(hb, u, vt, r2, n1, g1, g2, x2, row(final_norm_w), tt=512, te=2048, sub=512)
    return out.reshape(b, t, d)
```

```python
import functools

import jax
import jax.numpy as jnp
from jax import lax
from jax.experimental import pallas as pl
from jax.experimental.pallas import tpu as pltpu

EPS = 1e-6
CONV_WIDTH = 512
HGRN_HEADS = 4
HGRN_HEAD_DIM = 128
HGRN_WIDTH = HGRN_HEADS * HGRN_HEAD_DIM
CHUNK = 64
MEM_HEADS = 4
PEER_HEADS = 8
PEER_N_KEYS = 128
PEER_DK = 128
PEER_TOPK = 16
V7X_LANES = 128
BF16_SUBLANES = 16
V7X_VMEM_LIMIT = 56 * 1024 * 1024

BF16 = jnp.bfloat16
F32 = jnp.float32
NT_DIMS = (((1,), (1,)), ((), ()))
TN_DIMS = (((0,), (0,)), ((), ()))


def _rms(x, w):
    return x * lax.rsqrt(jnp.mean(x * x, axis=-1, keepdims=True) + EPS) * w


def _sigmoid(x):
    return 1.0 / (1.0 + jnp.exp(-x))


def _mixer_kernel(x_ref, nw_ref, win_ref, convw_ref, cnw_ref, lbl_ref, hnw_ref, wout_ref, o_ref,
                  proj_ref, ohg_ref, st_ref, halo_ref, *, tm):
    cw, hw, dh = CONV_WIDTH, HGRN_WIDTH, HGRN_HEAD_DIM

    @pl.when(pl.program_id(1) == 0)
    def _():
        st_ref[...] = jnp.zeros_like(st_ref)
        halo_ref[...] = jnp.zeros_like(halo_ref)

    x = x_ref[0]
    h = _rms(x, nw_ref[...]).astype(BF16)
    proj_ref[...] = jnp.dot(h, win_ref[...], preferred_element_type=F32)

    bg = proj_ref[:, 0:cw]
    u = proj_ref[:, cw:2 * cw] * proj_ref[:, 2 * cw:3 * cw]
    row = lax.broadcasted_iota(jnp.int32, (tm, cw), 0)
    prev = halo_ref[...]
    u1 = jnp.where(row == 0, prev[7:8, :], pltpu.roll(u, 1, 0))
    u2 = jnp.where(row == 0, prev[6:7, :], jnp.where(row == 1, prev[7:8, :], pltpu.roll(u, 2, 0)))
    halo_ref[...] = u[tm - 8:tm, :]
    cwt = convw_ref[...]
    yc = _rms(bg * (u2 * cwt[0:1, :] + u1 * cwt[1:2, :] + u * cwt[2:3, :]), cnw_ref[...])

    lbl = lbl_ref[...]
    lmax = jnp.max(lbl, axis=0, keepdims=True)
    lexp = jnp.exp(lbl - lmax)
    lb = lexp[0:1, :] / jnp.sum(lexp, axis=0, keepdims=True)
    o0 = 3 * cw
    rowc = lax.broadcasted_iota(jnp.int32, (CHUNK, hw), 0)
    tril = (lax.broadcasted_iota(jnp.int32, (CHUNK, CHUNK), 0)
            >= lax.broadcasted_iota(jnp.int32, (CHUNK, CHUNK), 1))
    for c in range(tm // CHUNK):
        r0 = c * CHUNK
        qc = proj_ref[r0:r0 + CHUNK, o0:o0 + hw]
        fc = proj_ref[r0:r0 + CHUNK, o0 + hw:o0 + 2 * hw]
        ic = proj_ref[r0:r0 + CHUNK, o0 + 2 * hw:o0 + 3 * hw]
        qa = qc * _sigmoid(qc)
        fg = lb + (1.0 - lb) * _sigmoid(fc)
        kk = 1.0 - fg
        g = jnp.log(fg)
        for s in (1, 2, 4, 8, 16, 32):
            g = g + jnp.where(rowc >= s, pltpu.roll(g, s, 0), 0.0)
        gl = g[CHUNK - 1:CHUNK, :]
        q_dec = (qa * jnp.exp(g)).astype(BF16)
        k_inv = (kk * jnp.exp(-g)).astype(BF16)
        k_end = (kk * jnp.exp(gl - g)).astype(BF16)
        dec = jnp.exp(gl)
        vb = ic.astype(BF16)
        for hd in range(HGRN_HEADS):
            cs = slice(hd * dh, (hd + 1) * dh)
            a = lax.dot_general(q_dec[:, cs], k_inv[:, cs], NT_DIMS, preferred_element_type=F32)
            a = jnp.where(tril, a, 0.0).astype(BF16)
            st = st_ref[hd]
            o = (jnp.dot(a, vb[:, cs], preferred_element_type=F32)
                 + lax.dot_general(q_dec[:, cs], st.astype(BF16), NT_DIMS, preferred_element_type=F32))
            ohg_ref[r0:r0 + CHUNK, cs] = o
            ut = lax.dot_general(vb[:, cs], k_end[:, cs], TN_DIMS, preferred_element_type=F32)
            st_ref[hd] = st * dec[:, cs] + ut

    gate = proj_ref[:, o0 + 3 * hw:o0 + 4 * hw]
    gate = gate * _sigmoid(gate)
    hnw = hnw_ref[...]
    acc = x + jnp.dot(yc.astype(BF16), wout_ref[0:cw, :], preferred_element_type=F32)
    for hd in range(HGRN_HEADS):
        cs = slice(hd * dh, (hd + 1) * dh)
        oh = ohg_ref[:, cs]
        oh = oh * lax.rsqrt(jnp.mean(oh * oh, axis=-1, keepdims=True) + EPS)
        yh = oh * hnw[:, cs] * gate[:, cs]
        acc = acc + jnp.dot(yh.astype(BF16), wout_ref[cw + hd * dh:cw + (hd + 1) * dh, :],
                            preferred_element_type=F32)
    o_ref[0] = acc


def _mixer(x, mix_norm_w, w_in, conv_w, conv_norm_w, lb_logits, hgrn_norm_w, w_out, *, tm):
    b, t, d = x.shape
    ncols = w_in.shape[1]
    const = lambda *_: (0, 0)
    return pl.pallas_call(
        functools.partial(_mixer_kernel, tm=tm),
        grid=(b, t // tm),
        in_specs=[
            pl.BlockSpec((1, tm, d), lambda i, j: (i, j, 0)),
            pl.BlockSpec((1, d), const),
            pl.BlockSpec((d, ncols), const),
            pl.BlockSpec(conv_w.shape, const),
            pl.BlockSpec((1, CONV_WIDTH), const),
            pl.BlockSpec(lb_logits.shape, const),
            pl.BlockSpec((1, HGRN_WIDTH), const),
            pl.BlockSpec((d, d), const),
        ],
        out_specs=pl.BlockSpec((1, tm, d), lambda i, j: (i, j, 0)),
        out_shape=jax.ShapeDtypeStruct((b, t, d), F32),
        scratch_shapes=[
            pltpu.VMEM((tm, ncols), F32),
            pltpu.VMEM((tm, HGRN_WIDTH), F32),
            pltpu.VMEM((HGRN_HEADS, HGRN_HEAD_DIM, HGRN_HEAD_DIM), F32),
            pltpu.VMEM((8, CONV_WIDTH), F32),
        ],
        compiler_params=pltpu.CompilerParams(
            dimension_semantics=("arbitrary", "arbitrary"), vmem_limit_bytes=V7X_VMEM_LIMIT),
        name="mixer",
    )(x, mix_norm_w, w_in, conv_w, conv_norm_w, lb_logits, hgrn_norm_w, w_out)


def _kv_kernel(mem_ref, nw_ref, wkv_ref, o_ref):
    mn = _rms(mem_ref[0], nw_ref[...]).astype(BF16)
    o_ref[0] = jnp.dot(mn, wkv_ref[...], preferred_element_type=F32).astype(BF16)


def _kv(mem, mem_norm_w, wkv):
    b, m, d = mem.shape
    return pl.pallas_call(
        _kv_kernel,
        grid=(b,),
        in_specs=[
            pl.BlockSpec((1, m, d), lambda i: (i, 0, 0)),
            pl.BlockSpec((1, d), lambda i: (0, 0)),
            pl.BlockSpec((d, 2 * d), lambda i: (0, 0)),
        ],
        out_specs=pl.BlockSpec((1, m, 2 * d), lambda i: (i, 0, 0)),
        out_shape=jax.ShapeDtypeStruct((b, m, 2 * d), BF16),
        compiler_params=pltpu.CompilerParams(
            dimension_semantics=("arbitrary",), vmem_limit_bytes=V7X_VMEM_LIMIT),
        name="kv",
    )(mem, mem_norm_w, wkv)


def _xattn_kernel(x_ref, nw_ref, wq_ref, kv_ref, wo_ref, o_ref):
    x = x_ref[0]
    d = x.shape[-1]
    hd_dim = d // MEM_HEADS
    h = _rms(x, nw_ref[...]).astype(BF16)
    q = jnp.dot(h, wq_ref[...], preferred_element_type=F32)
    acc = x
    for hd in range(MEM_HEADS):
        cs = slice(hd * hd_dim, (hd + 1) * hd_dim)
        kh = kv_ref[0, :, cs]
        vh = kv_ref[0, :, d + hd * hd_dim:d + (hd + 1) * hd_dim]
        s = lax.dot_general(q[:, cs].astype(BF16), kh, NT_DIMS, preferred_element_type=F32)
        s = s * (hd_dim ** -0.5)
        p = jnp.exp(s - jnp.max(s, axis=-1, keepdims=True))
        p = p / jnp.sum(p, axis=-1, keepdims=True)
        oh = jnp.dot(p.astype(BF16), vh, preferred_element_type=F32)
        acc = acc + jnp.dot(oh.astype(BF16), wo_ref[cs, :], preferred_element_type=F32)
    o_ref[0] = acc


def _xattn(x, xattn_norm_w, wq, kv, wo, *, tm):
    b, t, d = x.shape
    m = kv.shape[1]
    const = lambda *_: (0, 0)
    return pl.pallas_call(
        _xattn_kernel,
        grid=(b, t // tm),
        in_specs=[
            pl.BlockSpec((1, tm, d), lambda i, j: (i, j, 0)),
            pl.BlockSpec((1, d), const),
            pl.BlockSpec((d, d), const),
            pl.BlockSpec((1, m, 2 * d), lambda i, j: (i, 0, 0)),
            pl.BlockSpec((d, d), const),
        ],
        out_specs=pl.BlockSpec((1, tm, d), lambda i, j: (i, j, 0)),
        out_shape=jax.ShapeDtypeStruct((b, t, d), F32),
        compiler_params=pltpu.CompilerParams(
            dimension_semantics=("arbitrary", "arbitrary"), vmem_limit_bytes=V7X_VMEM_LIMIT),
        name="xattn",
    )(x, xattn_norm_w, wq, kv, wo)


def _staircase():
    return [(q, PEER_TOPK // (q + 1)) for q in range(PEER_TOPK)]


N_CAND = sum(n for _, n in _staircase())
N_CAND_PAD = -(-N_CAND // 8) * 8


def _top_values(s, vals_ref):
    work = s
    rank = jnp.full(s.shape, float(PEER_TOPK), F32)
    for r in range(PEER_TOPK):
        m = jnp.max(work, axis=0, keepdims=True)
        eq = work == m
        rank = jnp.where(eq, float(r), rank)
        work = jnp.where(eq, -jnp.inf, work)
        vals_ref[r:r + 1, :] = m
    return rank


def _bf16_bits(x):
    return pltpu.bitcast(x.astype(BF16).astype(F32), jnp.uint32)


def _dup_bf16_words(x):
    bits = _bf16_bits(x)
    return bits | lax.shift_right_logical(bits, jnp.uint32(16))


def _store_row_pairs(dst, x, slab_ref):
    rows, cols = x.shape
    for c in range(cols // V7X_LANES):
        cs = slice(c * V7X_LANES, (c + 1) * V7X_LANES)
        slab_ref[c] = x[:, cs]
        even = slab_ref[c, pl.ds(0, rows // 2, stride=2), :]
        odd = slab_ref[c, pl.ds(1, rows // 2, stride=2), :]
        dst[:, cs] = lax.shift_right_logical(_bf16_bits(even), jnp.uint32(16)) | _bf16_bits(odd)


def _route_kernel(x_ref, nw_ref, wqry_ref, keys_ref, hb_ref, r2_ref, n1_ref, g1_ref, g2_ref,
                  a_ref, b_ref, cand_ref, hslab_ref, kslab_ref):
    hf = _rms(x_ref[...], nw_ref[...])
    tm = hf.shape[0]
    _store_row_pairs(hb_ref, hf, hslab_ref)
    q = jnp.dot(hf.astype(BF16), wqry_ref[...], preferred_element_type=F32).astype(BF16)
    for hd in range(PEER_HEADS):
        c1 = (2 * hd) * PEER_DK
        c2 = (2 * hd + 1) * PEER_DK
        s1 = lax.dot_general(keys_ref[2 * hd], q[:, c1:c1 + PEER_DK], NT_DIMS, preferred_element_type=F32)
        s2 = lax.dot_general(keys_ref[2 * hd + 1], q[:, c2:c2 + PEER_DK], NT_DIMS, preferred_element_type=F32)
        _top_values(s1, a_ref)
        rank2 = _top_values(s2, b_ref)
        off = 0
        for qq, n in _staircase():
            cand_ref[off:off + n, :] = a_ref[0:n, :] + b_ref[qq:qq + 1, :]
            off += n
        if N_CAND_PAD > N_CAND:
            cand_ref[N_CAND:N_CAND_PAD, :] = jnp.full((N_CAND_PAD - N_CAND, tm), -jnp.inf, F32)
        cand = cand_ref[...]
        work = cand
        tau = None
        for _ in range(PEER_TOPK):
            tau = jnp.max(work, axis=0, keepdims=True)
            work = jnp.where(work == tau, -jnp.inf, work)
        cmax = cand[0:1, :]
        z = jnp.sum(jnp.where(cand >= tau, jnp.exp(cand - cmax), 0.0), axis=0, keepdims=True)
        n1 = jnp.zeros(s1.shape, F32)
        for qq in range(PEER_TOPK):
            n1 = n1 + jnp.where(s1 + b_ref[qq:qq + 1, :] >= tau, 1.0, 0.0)
        _store_row_pairs(r2_ref.at[hd], rank2, kslab_ref)
        n1_ref[hd] = _dup_bf16_words(n1)
        g1_ref[hd] = _dup_bf16_words(jnp.exp(s1 - a_ref[0:1, :]))
        _store_row_pairs(g2_ref.at[hd], jnp.exp(s2 - b_ref[0:1, :]) / z, kslab_ref)


def _route(x2, ffn_norm_w, w_query, keys, *, tm):
    n, d = x2.shape
    nq = w_query.shape[1]
    tab = jax.ShapeDtypeStruct((PEER_HEADS, PEER_N_KEYS, n), jnp.uint32)
    tabh = jax.ShapeDtypeStruct((PEER_HEADS, PEER_N_KEYS // 2, n), jnp.uint32)
    tab_spec = pl.BlockSpec((PEER_HEADS, PEER_N_KEYS, tm), lambda i: (0, 0, i))
    tabh_spec = pl.BlockSpec((PEER_HEADS, PEER_N_KEYS // 2, tm), lambda i: (0, 0, i))
    return pl.pallas_call(
        _route_kernel,
        grid=(n // tm,),
        in_specs=[
            pl.BlockSpec((tm, d), lambda i: (i, 0)),
            pl.BlockSpec((1, d), lambda i: (0, 0)),
            pl.BlockSpec((d, nq), lambda i: (0, 0)),
            pl.BlockSpec(keys.shape, lambda i: (0, 0, 0)),
        ],
        out_specs=[pl.BlockSpec((tm // 2, d), lambda i: (i, 0)), tabh_spec, tab_spec, tab_spec, tabh_spec],
        out_shape=[jax.ShapeDtypeStruct((n // 2, d), jnp.uint32), tabh, tab, tab, tabh],
        scratch_shapes=[
            pltpu.VMEM((PEER_TOPK, tm), F32),
            pltpu.VMEM((PEER_TOPK, tm), F32),
            pltpu.VMEM((N_CAND_PAD, tm), F32),
            pltpu.VMEM((d // V7X_LANES, tm, V7X_LANES), F32),
            pltpu.VMEM((tm // V7X_LANES, PEER_N_KEYS, V7X_LANES), F32),
        ],
        compiler_params=pltpu.CompilerParams(
            dimension_semantics=("arbitrary",), vmem_limit_bytes=V7X_VMEM_LIMIT),
        name="route",
    )(x2, ffn_norm_w, w_query, keys)


def _tables_kernel(down_ref, up_ref, u_ref, vt_ref, uslab_ref, vslab_ref):
    _store_row_pairs(u_ref, down_ref[...], uslab_ref)
    _store_row_pairs(vt_ref, up_ref[...].T, vslab_ref)


def _tables(peer_down, peer_up, *, te):
    ne, d = peer_down.shape
    return pl.pallas_call(
        _tables_kernel,
        grid=(ne // te,),
        in_specs=[pl.BlockSpec((te, d), lambda i: (i, 0)), pl.BlockSpec((te, d), lambda i: (i, 0))],
        out_specs=[pl.BlockSpec((te // 2, d), lambda i: (i, 0)), pl.BlockSpec((d // 2, te), lambda i: (0, i))],
        out_shape=[jax.ShapeDtypeStruct((ne // 2, d), jnp.uint32), jax.ShapeDtypeStruct((d // 2, ne), jnp.uint32)],
        scratch_shapes=[pltpu.VMEM((d // V7X_LANES, te, V7X_LANES), F32),
                        pltpu.VMEM((te // V7X_LANES, d, V7X_LANES), F32)],
        compiler_params=pltpu.CompilerParams(
            dimension_semantics=("arbitrary",), vmem_limit_bytes=V7X_VMEM_LIMIT),
        name="tables",
    )(peer_down, peer_up)


def _peer_kernel(hb_ref, u_ref, vt_ref, r2_ref, n1_ref, g1_ref, g2_ref, x2_ref, fnw_ref, o_ref,
                 s0_ref, s1_ref, h0_ref, h1_ref, acc_ref, *, tt, te, n_etiles):
    k = pl.program_id(0)
    kc = k - 2
    nk = PEER_N_KEYS
    lw = 2 * V7X_LANES

    @pl.when(k == 0)
    def _():
        for ref in (s0_ref, s1_ref, h0_ref, h1_ref):
            ref[...] = jnp.zeros_like(ref)

    @pl.when((k == 0) | ((kc >= 0) & (lax.rem(kc, n_etiles) == 0)))
    def _():
        acc_ref[...] = jnp.zeros_like(acc_ref)

    mc = 2 * nk
    d = acc_ref.shape[0]
    pieces_a = [(m, l) for l in range(tt // lw) for m in range(te // mc)]
    pieces_c = [(m, l) for l in range(tt // lw) for m in range(d // mc)]
    nb_rows = nk // 2
    ii_group = 2
    blocks_b = [(ig, jh, l) for l in range(tt // lw) for ig in range(te // nk // ii_group)
                for jh in range(nk // nb_rows)]

    def words(ref, *idx):
        return pltpu.bitcast(ref[idx], BF16)

    def stages(sa_ref, sb_ref, hb_out_ref, hc_ref):
        def stage_a(m, l):
            rs, ls = slice(m * mc, (m + 1) * mc), slice(l * lw, (l + 1) * lw)
            u = words(u_ref, slice(m * mc // 2, (m + 1) * mc // 2), slice(None))
            hb = words(hb_ref, slice(l * lw // 2, (l + 1) * lw // 2), slice(None))
            sa_ref[rs, ls] = lax.dot_general(u, hb, NT_DIMS, preferred_element_type=F32)

        def stage_b(ig, jh, l):
            ls = slice(l * lw, (l + 1) * lw)
            js = slice(jh * nb_rows // 2, (jh + 1) * nb_rows // 2)
            iis = range(ig * ii_group, (ig + 1) * ii_group)
            w = [jnp.zeros((nb_rows, lw), BF16) for _ in iis]
            for hd in range(PEER_HEADS):
                g2 = words(g2_ref, hd, js, ls)
                r2 = words(r2_ref, hd, js, ls)
                for t, ii in enumerate(iis):
                    n1 = pltpu.bitcast(jnp.broadcast_to(n1_ref[hd, ii:ii + 1, ls], (nb_rows // 2, lw)), BF16)
                    g1 = pltpu.bitcast(jnp.broadcast_to(g1_ref[hd, ii:ii + 1, ls], (nb_rows // 2, lw)), BF16)
                    w[t] = w[t] + jnp.where(r2 < n1, g2 * g1, jnp.zeros_like(g2))
            for t, ii in enumerate(iis):
                rs = slice(ii * nk + jh * nb_rows, ii * nk + (jh + 1) * nb_rows)
                s = sb_ref[rs, ls]
                act = 0.5 * s * (1.0 + lax.erf(s * 0.7071067811865476))
                hb_out_ref[rs, ls] = w[t] * act.astype(BF16)

        def stage_c(m, l):
            rs, ls = slice(m * mc, (m + 1) * mc), slice(l * lw, (l + 1) * lw)
            vt = words(vt_ref, slice(m * mc // 2, (m + 1) * mc // 2), slice(None))
            acc_ref[rs, ls] += jnp.dot(vt, hc_ref[:, ls], preferred_element_type=F32)

        nb = len(blocks_b) // len(pieces_a)
        for p in range(len(pieces_a)):
            for q in range(nb):
                stage_b(*blocks_b[p * nb + q])
            stage_a(*pieces_a[p])
            stage_c(*pieces_c[p])

    @pl.when(lax.rem(k, 2) == 0)
    def _():
        stages(s0_ref, s1_ref, h1_ref, h0_ref)

    @pl.when(lax.rem(k, 2) == 1)
    def _():
        stages(s1_ref, s0_ref, h0_ref, h1_ref)

    @pl.when((kc >= 0) & (lax.rem(kc, n_etiles) == n_etiles - 1))
    def _():
        z = x2_ref[...] + acc_ref[...].T
        o_ref[...] = _rms(z, fnw_ref[...])


def _peer(hb, u, vt, r2, n1, g1, g2, x2, final_norm_w, *, tt, te):
    n, d = x2.shape
    ne = vt.shape[1]
    n_ttiles, n_etiles = n // tt, ne // te
    n_tiles = n_ttiles * n_etiles
    ni = te // PEER_N_KEYS
    assert ni == 8, "the first-half key rows of one expert tile must be one 32-bit sublane group"

    def tile(k, lag):
        t = jnp.clip(k - lag, 0, n_tiles - 1)
        return t // n_etiles, lax.rem(t, n_etiles)

    tab_spec = pl.BlockSpec((PEER_HEADS, PEER_N_KEYS // 2, tt), lambda k: (0, 0, tile(k, 1)[0]))
    row_spec = pl.BlockSpec((PEER_HEADS, ni, tt), lambda k: (0, tile(k, 1)[1], tile(k, 1)[0]))
    return pl.pallas_call(
        functools.partial(_peer_kernel, tt=tt, te=te, n_etiles=n_etiles),
        grid=(n_tiles + 2,),
        in_specs=[
            pl.BlockSpec((tt // 2, d), lambda k: (tile(k, 0)[0], 0)),
            pl.BlockSpec((te // 2, d), lambda k: (tile(k, 0)[1], 0)),
            pl.BlockSpec((d // 2, te), lambda k: (0, tile(k, 2)[1])),
            tab_spec, row_spec, row_spec, tab_spec,
            pl.BlockSpec((tt, d), lambda k: (tile(k, 2)[0], 0)),
            pl.BlockSpec((1, d), lambda k: (0, 0)),
        ],
        out_specs=pl.BlockSpec((tt, d), lambda k: (tile(k, 2)[0], 0)),
        out_shape=jax.ShapeDtypeStruct((n, d), F32),
        scratch_shapes=[
            pltpu.VMEM((te, tt), F32),
            pltpu.VMEM((te, tt), F32),
            pltpu.VMEM((te, tt), BF16),
            pltpu.VMEM((te, tt), BF16),
            pltpu.VMEM((d, tt), F32),
        ],
        compiler_params=pltpu.CompilerParams(
            dimension_semantics=("arbitrary",), vmem_limit_bytes=V7X_VMEM_LIMIT),
        name="peer",
    )(hb, u, vt, r2, n1, g1, g2, x2, final_norm_w)


def kernel(x, mem, mix_norm_w, w_in, conv_w, conv_norm_w, hgrn_lb_logits, hgrn_norm_w, w_out, xattn_norm_w,
           mem_norm_w, wq_mem, wkv_mem, wo_mem, ffn_norm_w, peer_w_query, peer_sub_keys, peer_down, peer_up,
           final_norm_w):
    b, t, d = x.shape
    depth = w_in.shape[0]
    assert depth == 1, "single-layer problem"
    assert t % 512 == 0 and d % V7X_LANES == 0
    row = lambda v: v.reshape(1, -1)
    x1 = _mixer(x, row(mix_norm_w[0]), w_in[0].astype(BF16), conv_w[0], row(conv_norm_w[0]), hgrn_lb_logits,
                row(hgrn_norm_w[0]), w_out[0].astype(BF16), tm=512)
    kv = _kv(mem, row(mem_norm_w[0]), wkv_mem[0].astype(BF16))
    x2 = _xattn(x1, row(xattn_norm_w[0]), wq_mem[0].astype(BF16), kv, wo_mem[0].astype(BF16), tm=512)
    x2 = x2.reshape(b * t, d)
    keys = peer_sub_keys[0].reshape(2 * PEER_HEADS, PEER_N_KEYS, PEER_DK).astype(BF16)
    hb, r2, n1, g1, g2 = _route(x2, row(ffn_norm_w[0]), peer_w_query[0].astype(BF16), keys, tm=512)
    u, vt = _tables(peer_down[0], peer_up[0], te=512)
    out = _peer(hb, u, vt, r2, n1, g1, g2, x2, row(final_norm_w), tt=512, te=1024)
    return out.reshape(b, t, d)
```

```python
import functools

import jax
import jax.numpy as jnp
from jax import lax
from jax.experimental import pallas as pl
from jax.experimental.pallas import tpu as pltpu

EPS = 1e-6
CONV_WIDTH = 512
HGRN_HEADS = 4
HGRN_HEAD_DIM = 128
HGRN_WIDTH = HGRN_HEADS * HGRN_HEAD_DIM
CHUNK = 64
MEM_HEADS = 4
PEER_HEADS = 8
PEER_N_KEYS = 128
PEER_DK = 128
PEER_TOPK = 16
V7X_LANES = 128
BF16_SUBLANES = 16
V7X_VMEM_LIMIT = 56 * 1024 * 1024

BF16 = jnp.bfloat16
F32 = jnp.float32
NT_DIMS = (((1,), (1,)), ((), ()))
TN_DIMS = (((0,), (0,)), ((), ()))


def _rms(x, w):
    return x * lax.rsqrt(jnp.mean(x * x, axis=-1, keepdims=True) + EPS) * w


def _sigmoid(x):
    return 1.0 / (1.0 + jnp.exp(-x))


def _mixer_kernel(x_ref, nw_ref, win_ref, convw_ref, cnw_ref, lbl_ref, hnw_ref, wout_ref, o_ref,
                  proj_ref, ohg_ref, st_ref, halo_ref, *, tm):
    cw, hw, dh = CONV_WIDTH, HGRN_WIDTH, HGRN_HEAD_DIM

    @pl.when(pl.program_id(1) == 0)
    def _():
        st_ref[...] = jnp.zeros_like(st_ref)
        halo_ref[...] = jnp.zeros_like(halo_ref)

    x = x_ref[0]
    h = _rms(x, nw_ref[...]).astype(BF16)
    proj_ref[...] = jnp.dot(h, win_ref[...], preferred_element_type=F32)

    bg = proj_ref[:, 0:cw]
    u = proj_ref[:, cw:2 * cw] * proj_ref[:, 2 * cw:3 * cw]
    row = lax.broadcasted_iota(jnp.int32, (tm, cw), 0)
    prev = halo_ref[...]
    u1 = jnp.where(row == 0, prev[7:8, :], pltpu.roll(u, 1, 0))
    u2 = jnp.where(row == 0, prev[6:7, :], jnp.where(row == 1, prev[7:8, :], pltpu.roll(u, 2, 0)))
    halo_ref[...] = u[tm - 8:tm, :]
    cwt = convw_ref[...]
    yc = _rms(bg * (u2 * cwt[0:1, :] + u1 * cwt[1:2, :] + u * cwt[2:3, :]), cnw_ref[...])

    lbl = lbl_ref[...]
    lmax = jnp.max(lbl, axis=0, keepdims=True)
    lexp = jnp.exp(lbl - lmax)
    lb = lexp[0:1, :] / jnp.sum(lexp, axis=0, keepdims=True)
    o0 = 3 * cw
    rowc = lax.broadcasted_iota(jnp.int32, (CHUNK, hw), 0)
    tril = (lax.broadcasted_iota(jnp.int32, (CHUNK, CHUNK), 0)
            >= lax.broadcasted_iota(jnp.int32, (CHUNK, CHUNK), 1))
    for c in range(tm // CHUNK):
        r0 = c * CHUNK
        qc = proj_ref[r0:r0 + CHUNK, o0:o0 + hw]
        fc = proj_ref[r0:r0 + CHUNK, o0 + hw:o0 + 2 * hw]
        ic = proj_ref[r0:r0 + CHUNK, o0 + 2 * hw:o0 + 3 * hw]
        qa = qc * _sigmoid(qc)
        fg = lb + (1.0 - lb) * _sigmoid(fc)
        kk = 1.0 - fg
        g = jnp.log(fg)
        for s in (1, 2, 4, 8, 16, 32):
            g = g + jnp.where(rowc >= s, pltpu.roll(g, s, 0), 0.0)
        gl = g[CHUNK - 1:CHUNK, :]
        q_dec = (qa * jnp.exp(g)).astype(BF16)
        k_inv = (kk * jnp.exp(-g)).astype(BF16)
        k_end = (kk * jnp.exp(gl - g)).astype(BF16)
        dec = jnp.exp(gl)
        vb = ic.astype(BF16)
        for hd in range(HGRN_HEADS):
            cs = slice(hd * dh, (hd + 1) * dh)
            a = lax.dot_general(q_dec[:, cs], k_inv[:, cs], NT_DIMS, preferred_element_type=F32)
            a = jnp.where(tril, a, 0.0).astype(BF16)
            st = st_ref[hd]
            o = (jnp.dot(a, vb[:, cs], preferred_element_type=F32)
                 + lax.dot_general(q_dec[:, cs], st.astype(BF16), NT_DIMS, preferred_element_type=F32))
            ohg_ref[r0:r0 + CHUNK, cs] = o
            ut = lax.dot_general(vb[:, cs], k_end[:, cs], TN_DIMS, preferred_element_type=F32)
            st_ref[hd] = st * dec[:, cs] + ut

    gate = proj_ref[:, o0 + 3 * hw:o0 + 4 * hw]
    gate = gate * _sigmoid(gate)
    hnw = hnw_ref[...]
    acc = x + jnp.dot(yc.astype(BF16), wout_ref[0:cw, :], preferred_element_type=F32)
    for hd in range(HGRN_HEADS):
        cs = slice(hd * dh, (hd + 1) * dh)
        oh = ohg_ref[:, cs]
        oh = oh * lax.rsqrt(jnp.mean(oh * oh, axis=-1, keepdims=True) + EPS)
        yh = oh * hnw[:, cs] * gate[:, cs]
        acc = acc + jnp.dot(yh.astype(BF16), wout_ref[cw + hd * dh:cw + (hd + 1) * dh, :],
                            preferred_element_type=F32)
    o_ref[0] = acc


def _mixer(x, mix_norm_w, w_in, conv_w, conv_norm_w, lb_logits, hgrn_norm_w, w_out, *, tm):
    b, t, d = x.shape
    ncols = w_in.shape[1]
    const = lambda *_: (0, 0)
    return pl.pallas_call(
        functools.partial(_mixer_kernel, tm=tm),
        grid=(b, t // tm),
        in_specs=[
            pl.BlockSpec((1, tm, d), lambda i, j: (i, j, 0)),
            pl.BlockSpec((1, d), const),
            pl.BlockSpec((d, ncols), const),
            pl.BlockSpec(conv_w.shape, const),
            pl.BlockSpec((1, CONV_WIDTH), const),
            pl.BlockSpec(lb_logits.shape, const),
            pl.BlockSpec((1, HGRN_WIDTH), const),
            pl.BlockSpec((d, d), const),
        ],
        out_specs=pl.BlockSpec((1, tm, d), lambda i, j: (i, j, 0)),
        out_shape=jax.ShapeDtypeStruct((b, t, d), F32),
        scratch_shapes=[
            pltpu.VMEM((tm, ncols), F32),
            pltpu.VMEM((tm, HGRN_WIDTH), F32),
            pltpu.VMEM((HGRN_HEADS, HGRN_HEAD_DIM, HGRN_HEAD_DIM), F32),
            pltpu.VMEM((8, CONV_WIDTH), F32),
        ],
        compiler_params=pltpu.CompilerParams(
            dimension_semantics=("arbitrary", "arbitrary"), vmem_limit_bytes=V7X_VMEM_LIMIT),
        name="mixer",
    )(x, mix_norm_w, w_in, conv_w, conv_norm_w, lb_logits, hgrn_norm_w, w_out)


def _kv_kernel(mem_ref, nw_ref, wkv_ref, o_ref):
    mn = _rms(mem_ref[0], nw_ref[...]).astype(BF16)
    o_ref[0] = jnp.dot(mn, wkv_ref[...], preferred_element_type=F32).astype(BF16)


def _kv(mem, mem_norm_w, wkv):
    b, m, d = mem.shape
    return pl.pallas_call(
        _kv_kernel,
        grid=(b,),
        in_specs=[
            pl.BlockSpec((1, m, d), lambda i: (i, 0, 0)),
            pl.BlockSpec((1, d), lambda i: (0, 0)),
            pl.BlockSpec((d, 2 * d), lambda i: (0, 0)),
        ],
        out_specs=pl.BlockSpec((1, m, 2 * d), lambda i: (i, 0, 0)),
        out_shape=jax.ShapeDtypeStruct((b, m, 2 * d), BF16),
        compiler_params=pltpu.CompilerParams(
            dimension_semantics=("arbitrary",), vmem_limit_bytes=V7X_VMEM_LIMIT),
        name="kv",
    )(mem, mem_norm_w, wkv)


def _xattn_kernel(x_ref, nw_ref, wq_ref, kv_ref, wo_ref, o_ref):
    x = x_ref[0]
    d = x.shape[-1]
    hd_dim = d // MEM_HEADS
    h = _rms(x, nw_ref[...]).astype(BF16)
    q = jnp.dot(h, wq_ref[...], preferred_element_type=F32)
    acc = x
    for hd in range(MEM_HEADS):
        cs = slice(hd * hd_dim, (hd + 1) * hd_dim)
        kh = kv_ref[0, :, cs]
        vh = kv_ref[0, :, d + hd * hd_dim:d + (hd + 1) * hd_dim]
        s = lax.dot_general(q[:, cs].astype(BF16), kh, NT_DIMS, preferred_element_type=F32)
        s = s * (hd_dim ** -0.5)
        p = jnp.exp(s - jnp.max(s, axis=-1, keepdims=True))
        p = p / jnp.sum(p, axis=-1, keepdims=True)
        oh = jnp.dot(p.astype(BF16), vh, preferred_element_type=F32)
        acc = acc + jnp.dot(oh.astype(BF16), wo_ref[cs, :], preferred_element_type=F32)
    o_ref[0] = acc


def _xattn(x, xattn_norm_w, wq, kv, wo, *, tm):
    b, t, d = x.shape
    m = kv.shape[1]
    const = lambda *_: (0, 0)
    return pl.pallas_call(
        _xattn_kernel,
        grid=(b, t // tm),
        in_specs=[
            pl.BlockSpec((1, tm, d), lambda i, j: (i, j, 0)),
            pl.BlockSpec((1, d), const),
            pl.BlockSpec((d, d), const),
            pl.BlockSpec((1, m, 2 * d), lambda i, j: (i, 0, 0)),
            pl.BlockSpec((d, d), const),
        ],
        out_specs=pl.BlockSpec((1, tm, d), lambda i, j: (i, j, 0)),
        out_shape=jax.ShapeDtypeStruct((b, t, d), F32),
        compiler_params=pltpu.CompilerParams(
            dimension_semantics=("arbitrary", "arbitrary"), vmem_limit_bytes=V7X_VMEM_LIMIT),
        name="xattn",
    )(x, xattn_norm_w, wq, kv, wo)


def _staircase():
    return [(q, PEER_TOPK // (q + 1)) for q in range(PEER_TOPK)]


N_CAND = sum(n for _, n in _staircase())
N_CAND_PAD = -(-N_CAND // 8) * 8


def _top_values(s, vals_ref, ls, *, with_rank):
    work = s
    rank = jnp.full(s.shape, float(PEER_TOPK), F32) if with_rank else None
    for r in range(PEER_TOPK):
        m = jnp.max(work, axis=0, keepdims=True)
        eq = work == m
        if with_rank:
            rank = jnp.where(eq, float(r), rank)
        work = jnp.where(eq, -jnp.inf, work)
        vals_ref[r:r + 1, ls] = m
    return rank


def _select(s1, s2, a_ref, b_ref, cand_ref, ls):
    _top_values(s1, a_ref, ls, with_rank=False)
    rank2 = _top_values(s2, b_ref, ls, with_rank=True)
    a = a_ref[:, ls]
    off = 0
    for qq, n in _staircase():
        cand_ref[off:off + n, ls] = a[0:n, :] + b_ref[qq:qq + 1, ls]
        off += n
    if N_CAND_PAD > N_CAND:
        cand_ref[N_CAND:N_CAND_PAD, ls] = jnp.full((N_CAND_PAD - N_CAND, a.shape[1]), -jnp.inf, F32)
    cand = cand_ref[:, ls]
    work = cand
    tau = None
    for _ in range(PEER_TOPK):
        tau = jnp.max(work, axis=0, keepdims=True)
        work = jnp.where(work == tau, -jnp.inf, work)
    z = jnp.sum(jnp.where(cand >= tau, jnp.exp(cand - cand[0:1, :]), 0.0), axis=0, keepdims=True)
    n1 = jnp.zeros(s1.shape, F32)
    for qq in range(PEER_TOPK):
        ok = a + b_ref[qq:qq + 1, ls] >= tau
        theta = jnp.min(jnp.where(ok, a, jnp.inf), axis=0, keepdims=True)
        n1 = n1 + jnp.where(s1 >= theta, 1.0, 0.0)
    g1 = jnp.exp(s1 - a[0:1, :])
    g2 = jnp.exp(s2 - b_ref[0:1, ls]) * (1.0 / z)
    return rank2, n1, g1, g2


def _bf16_bits(x):
    return pltpu.bitcast(x.astype(BF16).astype(F32), jnp.uint32)


def _dup_bf16_words(x):
    bits = _bf16_bits(x)
    return bits | lax.shift_right_logical(bits, jnp.uint32(16))


def _store_row_pairs(dst, x, slab_ref):
    rows, cols = x.shape
    for c in range(cols // V7X_LANES):
        cs = slice(c * V7X_LANES, (c + 1) * V7X_LANES)
        slab_ref[c] = x[:, cs]
        even = slab_ref[c, pl.ds(0, rows // 2, stride=2), :]
        odd = slab_ref[c, pl.ds(1, rows // 2, stride=2), :]
        dst[:, cs] = lax.shift_right_logical(_bf16_bits(even), jnp.uint32(16)) | _bf16_bits(odd)


def _route_kernel(x_ref, nw_ref, wqry_ref, keys_ref, hb_ref, r2_ref, n1_ref, g1_ref, g2_ref,
                  a_ref, b_ref, cand_ref, hslab_ref, kslab_ref, s1_ref, s2_ref):
    hf = _rms(x_ref[...], nw_ref[...])
    tm = hf.shape[0]
    _store_row_pairs(hb_ref, hf, hslab_ref)
    q = jnp.dot(hf.astype(BF16), wqry_ref[...], preferred_element_type=F32).astype(BF16)
    for hd in range(PEER_HEADS):
        c1 = (2 * hd) * PEER_DK
        c2 = (2 * hd + 1) * PEER_DK
        s1_ref[...] = lax.dot_general(keys_ref[2 * hd], q[:, c1:c1 + PEER_DK], NT_DIMS,
                                      preferred_element_type=F32)
        s2_ref[...] = lax.dot_general(keys_ref[2 * hd + 1], q[:, c2:c2 + PEER_DK], NT_DIMS,
                                      preferred_element_type=F32)
        for lt in range(tm // V7X_LANES):
            ls = slice(lt * V7X_LANES, (lt + 1) * V7X_LANES)
            rank2, n1, g1, g2 = _select(s1_ref[:, ls], s2_ref[:, ls], a_ref, b_ref, cand_ref, ls)
            _store_row_pairs(r2_ref.at[hd, :, ls], rank2, kslab_ref.at[pl.ds(2 * lt, 1)])
            n1_ref[hd, :, ls] = _dup_bf16_words(n1)
            g1_ref[hd, :, ls] = _dup_bf16_words(g1)
            _store_row_pairs(g2_ref.at[hd, :, ls], g2, kslab_ref.at[pl.ds(2 * lt + 1, 1)])


def _route(x2, ffn_norm_w, w_query, keys, *, tm):
    n, d = x2.shape
    nq = w_query.shape[1]
    tab = jax.ShapeDtypeStruct((PEER_HEADS, PEER_N_KEYS, n), jnp.uint32)
    tabh = jax.ShapeDtypeStruct((PEER_HEADS, PEER_N_KEYS // 2, n), jnp.uint32)
    tab_spec = pl.BlockSpec((PEER_HEADS, PEER_N_KEYS, tm), lambda i: (0, 0, i))
    tabh_spec = pl.BlockSpec((PEER_HEADS, PEER_N_KEYS // 2, tm), lambda i: (0, 0, i))
    return pl.pallas_call(
        _route_kernel,
        grid=(n // tm,),
        in_specs=[
            pl.BlockSpec((tm, d), lambda i: (i, 0)),
            pl.BlockSpec((1, d), lambda i: (0, 0)),
            pl.BlockSpec((d, nq), lambda i: (0, 0)),
            pl.BlockSpec(keys.shape, lambda i: (0, 0, 0)),
        ],
        out_specs=[pl.BlockSpec((tm // 2, d), lambda i: (i, 0)), tabh_spec, tab_spec, tab_spec, tabh_spec],
        out_shape=[jax.ShapeDtypeStruct((n // 2, d), jnp.uint32), tabh, tab, tab, tabh],
        scratch_shapes=[
            pltpu.VMEM((PEER_TOPK, tm), F32),
            pltpu.VMEM((PEER_TOPK, tm), F32),
            pltpu.VMEM((N_CAND_PAD, tm), F32),
            pltpu.VMEM((d // V7X_LANES, tm, V7X_LANES), F32),
            pltpu.VMEM((2 * tm // V7X_LANES, PEER_N_KEYS, V7X_LANES), F32),
            pltpu.VMEM((PEER_N_KEYS, tm), F32),
            pltpu.VMEM((PEER_N_KEYS, tm), F32),
        ],
        compiler_params=pltpu.CompilerParams(
            dimension_semantics=("arbitrary",), vmem_limit_bytes=V7X_VMEM_LIMIT),
        name="route",
    )(x2, ffn_norm_w, w_query, keys)


def _tables_kernel(down_ref, up_ref, u_ref, vt_ref, uslab_ref, vslab_ref):
    _store_row_pairs(u_ref, down_ref[...], uslab_ref)
    _store_row_pairs(vt_ref, up_ref[...].T, vslab_ref)


def _tables(peer_down, peer_up, *, te):
    ne, d = peer_down.shape
    return pl.pallas_call(
        _tables_kernel,
        grid=(ne // te,),
        in_specs=[pl.BlockSpec((te, d), lambda i: (i, 0)), pl.BlockSpec((te, d), lambda i: (i, 0))],
        out_specs=[pl.BlockSpec((te // 2, d), lambda i: (i, 0)), pl.BlockSpec((d // 2, te), lambda i: (0, i))],
        out_shape=[jax.ShapeDtypeStruct((ne // 2, d), jnp.uint32), jax.ShapeDtypeStruct((d // 2, ne), jnp.uint32)],
        scratch_shapes=[pltpu.VMEM((d // V7X_LANES, te, V7X_LANES), F32),
                        pltpu.VMEM((te // V7X_LANES, d, V7X_LANES), F32)],
        compiler_params=pltpu.CompilerParams(
            dimension_semantics=("arbitrary",), vmem_limit_bytes=V7X_VMEM_LIMIT),
        name="tables",
    )(peer_down, peer_up)


def _peer_kernel(hb_ref, u_ref, vt_ref, r2_ref, n1_ref, g1_ref, g2_ref, x2_ref, fnw_ref, o_ref,
                 s0_ref, s1_ref, h0_ref, h1_ref, acc_ref, *, tt, te, n_etiles):
    k = pl.program_id(0)
    kc = k - 2
    nk = PEER_N_KEYS
    lw = 2 * V7X_LANES

    @pl.when(k == 0)
    def _():
        for ref in (s0_ref, s1_ref, h0_ref, h1_ref):
            ref[...] = jnp.zeros_like(ref)

    @pl.when((k == 0) | ((kc >= 0) & (lax.rem(kc, n_etiles) == 0)))
    def _():
        acc_ref[...] = jnp.zeros_like(acc_ref)

    mc = 2 * nk
    d = acc_ref.shape[0]
    pieces_a = [(m, l) for l in range(tt // lw) for m in range(te // mc)]
    pieces_c = [(m, l) for l in range(tt // lw) for m in range(d // mc)]
    nb_rows = nk // 2
    ii_group = 2
    blocks_b = [(ig, jh, l) for l in range(tt // lw) for ig in range(te // nk // ii_group)
                for jh in range(nk // nb_rows)]

    def words(ref, *idx):
        return pltpu.bitcast(ref[idx], BF16)

    def stages(sa_ref, sb_ref, hb_out_ref, hc_ref):
        def stage_a(m, l):
            rs, ls = slice(m * mc, (m + 1) * mc), slice(l * lw, (l + 1) * lw)
            u = words(u_ref, slice(m * mc // 2, (m + 1) * mc // 2), slice(None))
            hb = words(hb_ref, slice(l * lw // 2, (l + 1) * lw // 2), slice(None))
            sa_ref[rs, ls] = lax.dot_general(u, hb, NT_DIMS, preferred_element_type=F32)

        def stage_b(ig, jh, l):
            ls = slice(l * lw, (l + 1) * lw)
            js = slice(jh * nb_rows // 2, (jh + 1) * nb_rows // 2)
            iis = range(ig * ii_group, (ig + 1) * ii_group)
            w = [jnp.zeros((nb_rows, lw), BF16) for _ in iis]
            for hd in range(PEER_HEADS):
                g2 = words(g2_ref, hd, js, ls)
                r2 = words(r2_ref, hd, js, ls)
                for t, ii in enumerate(iis):
                    n1 = pltpu.bitcast(jnp.broadcast_to(n1_ref[hd, ii:ii + 1, ls], (nb_rows // 2, lw)), BF16)
                    g1 = pltpu.bitcast(jnp.broadcast_to(g1_ref[hd, ii:ii + 1, ls], (nb_rows // 2, lw)), BF16)
                    w[t] = w[t] + jnp.where(r2 < n1, g2 * g1, jnp.zeros_like(g2))
            for t, ii in enumerate(iis):
                rs = slice(ii * nk + jh * nb_rows, ii * nk + (jh + 1) * nb_rows)
                s = sb_ref[rs, ls]
                act = 0.5 * s * (1.0 + lax.erf(s * 0.7071067811865476))
                hb_out_ref[rs, ls] = w[t] * act.astype(BF16)

        def stage_c(m, l):
            rs, ls = slice(m * mc, (m + 1) * mc), slice(l * lw, (l + 1) * lw)
            vt = words(vt_ref, slice(m * mc // 2, (m + 1) * mc // 2), slice(None))
            acc_ref[rs, ls] += jnp.dot(vt, hc_ref[:, ls], preferred_element_type=F32)

        nb = len(blocks_b) // len(pieces_a)
        for p in range(len(pieces_a)):
            for q in range(nb):
                stage_b(*blocks_b[p * nb + q])
            stage_a(*pieces_a[p])
            stage_c(*pieces_c[p])

    @pl.when(lax.rem(k, 2) == 0)
    def _():
        stages(s0_ref, s1_ref, h1_ref, h0_ref)

    @pl.when(lax.rem(k, 2) == 1)
    def _():
        stages(s1_ref, s0_ref, h0_ref, h1_ref)

    @pl.when((kc >= 0) & (lax.rem(kc, n_etiles) == n_etiles - 1))
    def _():
        z = x2_ref[...] + acc_ref[...].T
        o_ref[...] = _rms(z, fnw_ref[...])


def _peer(hb, u, vt, r2, n1, g1, g2, x2, final_norm_w, *, tt, te):
    n, d = x2.shape
    ne = vt.shape[1]
    n_ttiles, n_etiles = n // tt, ne // te
    n_tiles = n_ttiles * n_etiles
    ni = te // PEER_N_KEYS
    assert ni == 8, "the first-half key rows of one expert tile must be one 32-bit sublane group"

    def tile(k, lag):
        t = jnp.clip(k - lag, 0, n_tiles - 1)
        return t // n_etiles, lax.rem(t, n_etiles)

    tab_spec = pl.BlockSpec((PEER_HEADS, PEER_N_KEYS // 2, tt), lambda k: (0, 0, tile(k, 1)[0]))
    row_spec = pl.BlockSpec((PEER_HEADS, ni, tt), lambda k: (0, tile(k, 1)[1], tile(k, 1)[0]))
    return pl.pallas_call(
        functools.partial(_peer_kernel, tt=tt, te=te, n_etiles=n_etiles),
        grid=(n_tiles + 2,),
        in_specs=[
            pl.BlockSpec((tt // 2, d), lambda k: (tile(k, 0)[0], 0)),
            pl.BlockSpec((te // 2, d), lambda k: (tile(k, 0)[1], 0)),
            pl.BlockSpec((d // 2, te), lambda k: (0, tile(k, 2)[1])),
            tab_spec, row_spec, row_spec, tab_spec,
            pl.BlockSpec((tt, d), lambda k: (tile(k, 2)[0], 0), pipeline_mode=pl.Buffered(1)),
            pl.BlockSpec((1, d), lambda k: (0, 0)),
        ],
        out_specs=pl.BlockSpec((tt, d), lambda k: (tile(k, 2)[0], 0), pipeline_mode=pl.Buffered(1)),
        out_shape=jax.ShapeDtypeStruct((n, d), F32),
        scratch_shapes=[
            pltpu.VMEM((te, tt), F32),
            pltpu.VMEM((te, tt), F32),
            pltpu.VMEM((te, tt), BF16),
            pltpu.VMEM((te, tt), BF16),
            pltpu.VMEM((d, tt), F32),
        ],
        compiler_params=pltpu.CompilerParams(
            dimension_semantics=("arbitrary",), vmem_limit_bytes=V7X_VMEM_LIMIT),
        name="peer",
    )(hb, u, vt, r2, n1, g1, g2, x2, final_norm_w)


def kernel(x, mem, mix_norm_w, w_in, conv_w, conv_norm_w, hgrn_lb_logits, hgrn_norm_w, w_out, xattn_norm_w,
           mem_norm_w, wq_mem, wkv_mem, wo_mem, ffn_norm_w, peer_w_query, peer_sub_keys, peer_down, peer_up,
           final_norm_w):
    b, t, d = x.shape
    depth = w_in.shape[0]
    assert depth == 1, "single-layer problem"
    assert t % 512 == 0 and d % V7X_LANES == 0
    row = lambda v: v.reshape(1, -1)
    x1 = _mixer(x, row(mix_norm_w[0]), w_in[0].astype(BF16), conv_w[0], row(conv_norm_w[0]), hgrn_lb_logits,
                row(hgrn_norm_w[0]), w_out[0].astype(BF16), tm=512)
    kv = _kv(mem, row(mem_norm_w[0]), wkv_mem[0].astype(BF16))
    x2 = _xattn(x1, row(xattn_norm_w[0]), wq_mem[0].astype(BF16), kv, wo_mem[0].astype(BF16), tm=512)
    x2 = x2.reshape(b * t, d)
    keys = peer_sub_keys[0].reshape(2 * PEER_HEADS, PEER_N_KEYS, PEER_DK).astype(BF16)
    hb, r2, n1, g1, g2 = _route(x2, row(ffn_norm_w[0]), peer_w_query[0].astype(BF16), keys, tm=512)
    u, vt = _tables(peer_down[0], peer_up[0], te=512)
    out = _peer(hb, u, vt, r2, n1, g1, g2, x2, row(final_norm_w), tt=1024, te=1024)
    return out.reshape(b, t, d)
```

```python
import functools

import jax
import jax.numpy as jnp
from jax import lax
from jax.experimental import pallas as pl
from jax.experimental.pallas import tpu as pltpu

EPS = 1e-6
CONV_WIDTH = 512
HGRN_HEADS = 4
HGRN_HEAD_DIM = 128
HGRN_WIDTH = HGRN_HEADS * HGRN_HEAD_DIM
CHUNK = 64
MEM_HEADS = 4
PEER_HEADS = 8
PEER_N_KEYS = 128
PEER_DK = 128
PEER_TOPK = 16
V7X_LANES = 128
BF16_SUBLANES = 16
V7X_VMEM_LIMIT = 56 * 1024 * 1024

BF16 = jnp.bfloat16
F32 = jnp.float32
NT_DIMS = (((1,), (1,)), ((), ()))
TN_DIMS = (((0,), (0,)), ((), ()))


def _rms(x, w):
    return x * lax.rsqrt(jnp.mean(x * x, axis=-1, keepdims=True) + EPS) * w


def _sigmoid(x):
    return 1.0 / (1.0 + jnp.exp(-x))


def _mixer_kernel(x_ref, nw_ref, win_ref, convw_ref, cnw_ref, lbl_ref, hnw_ref, wout_ref, o_ref,
                  proj_ref, ohg_ref, st_ref, halo_ref, *, tm):
    cw, hw, dh = CONV_WIDTH, HGRN_WIDTH, HGRN_HEAD_DIM

    @pl.when(pl.program_id(1) == 0)
    def _():
        st_ref[...] = jnp.zeros_like(st_ref)
        halo_ref[...] = jnp.zeros_like(halo_ref)

    x = x_ref[0]
    h = _rms(x, nw_ref[...]).astype(BF16)
    proj_ref[...] = jnp.dot(h, win_ref[...], preferred_element_type=F32)

    bg = proj_ref[:, 0:cw]
    u = proj_ref[:, cw:2 * cw] * proj_ref[:, 2 * cw:3 * cw]
    row = lax.broadcasted_iota(jnp.int32, (tm, cw), 0)
    prev = halo_ref[...]
    u1 = jnp.where(row == 0, prev[7:8, :], pltpu.roll(u, 1, 0))
    u2 = jnp.where(row == 0, prev[6:7, :], jnp.where(row == 1, prev[7:8, :], pltpu.roll(u, 2, 0)))
    halo_ref[...] = u[tm - 8:tm, :]
    cwt = convw_ref[...]
    yc = _rms(bg * (u2 * cwt[0:1, :] + u1 * cwt[1:2, :] + u * cwt[2:3, :]), cnw_ref[...])

    lbl = lbl_ref[...]
    lmax = jnp.max(lbl, axis=0, keepdims=True)
    lexp = jnp.exp(lbl - lmax)
    lb = lexp[0:1, :] / jnp.sum(lexp, axis=0, keepdims=True)
    o0 = 3 * cw
    rowc = lax.broadcasted_iota(jnp.int32, (CHUNK, hw), 0)
    tril = (lax.broadcasted_iota(jnp.int32, (CHUNK, CHUNK), 0)
            >= lax.broadcasted_iota(jnp.int32, (CHUNK, CHUNK), 1))
    for c in range(tm // CHUNK):
        r0 = c * CHUNK
        qc = proj_ref[r0:r0 + CHUNK, o0:o0 + hw]
        fc = proj_ref[r0:r0 + CHUNK, o0 + hw:o0 + 2 * hw]
        ic = proj_ref[r0:r0 + CHUNK, o0 + 2 * hw:o0 + 3 * hw]
        qa = qc * _sigmoid(qc)
        fg = lb + (1.0 - lb) * _sigmoid(fc)
        kk = 1.0 - fg
        g = jnp.log(fg)
        for s in (1, 2, 4, 8, 16, 32):
            g = g + jnp.where(rowc >= s, pltpu.roll(g, s, 0), 0.0)
        gl = g[CHUNK - 1:CHUNK, :]
        q_dec = (qa * jnp.exp(g)).astype(BF16)
        k_inv = (kk * jnp.exp(-g)).astype(BF16)
        k_end = (kk * jnp.exp(gl - g)).astype(BF16)
        dec = jnp.exp(gl)
        vb = ic.astype(BF16)
        for hd in range(HGRN_HEADS):
            cs = slice(hd * dh, (hd + 1) * dh)
            a = lax.dot_general(q_dec[:, cs], k_inv[:, cs], NT_DIMS, preferred_element_type=F32)
            a = jnp.where(tril, a, 0.0).astype(BF16)
            st = st_ref[hd]
            o = (jnp.dot(a, vb[:, cs], preferred_element_type=F32)
                 + lax.dot_general(q_dec[:, cs], st.astype(BF16), NT_DIMS, preferred_element_type=F32))
            ohg_ref[r0:r0 + CHUNK, cs] = o
            ut = lax.dot_general(vb[:, cs], k_end[:, cs], TN_DIMS, preferred_element_type=F32)
            st_ref[hd] = st * dec[:, cs] + ut

    gate = proj_ref[:, o0 + 3 * hw:o0 + 4 * hw]
    gate = gate * _sigmoid(gate)
    hnw = hnw_ref[...]
    acc = x + jnp.dot(yc.astype(BF16), wout_ref[0:cw, :], preferred_element_type=F32)
    for hd in range(HGRN_HEADS):
        cs = slice(hd * dh, (hd + 1) * dh)
        oh = ohg_ref[:, cs]
        oh = oh * lax.rsqrt(jnp.mean(oh * oh, axis=-1, keepdims=True) + EPS)
        yh = oh * hnw[:, cs] * gate[:, cs]
        acc = acc + jnp.dot(yh.astype(BF16), wout_ref[cw + hd * dh:cw + (hd + 1) * dh, :],
                            preferred_element_type=F32)
    o_ref[0] = acc


def _mixer(x, mix_norm_w, w_in, conv_w, conv_norm_w, lb_logits, hgrn_norm_w, w_out, *, tm):
    b, t, d = x.shape
    ncols = w_in.shape[1]
    const = lambda *_: (0, 0)
    return pl.pallas_call(
        functools.partial(_mixer_kernel, tm=tm),
        grid=(b, t // tm),
        in_specs=[
            pl.BlockSpec((1, tm, d), lambda i, j: (i, j, 0)),
            pl.BlockSpec((1, d), const),
            pl.BlockSpec((d, ncols), const),
            pl.BlockSpec(conv_w.shape, const),
            pl.BlockSpec((1, CONV_WIDTH), const),
            pl.BlockSpec(lb_logits.shape, const),
            pl.BlockSpec((1, HGRN_WIDTH), const),
            pl.BlockSpec((d, d), const),
        ],
        out_specs=pl.BlockSpec((1, tm, d), lambda i, j: (i, j, 0)),
        out_shape=jax.ShapeDtypeStruct((b, t, d), F32),
        scratch_shapes=[
            pltpu.VMEM((tm, ncols), F32),
            pltpu.VMEM((tm, HGRN_WIDTH), F32),
            pltpu.VMEM((HGRN_HEADS, HGRN_HEAD_DIM, HGRN_HEAD_DIM), F32),
            pltpu.VMEM((8, CONV_WIDTH), F32),
        ],
        compiler_params=pltpu.CompilerParams(
            dimension_semantics=("arbitrary", "arbitrary"), vmem_limit_bytes=V7X_VMEM_LIMIT),
        name="mixer",
    )(x, mix_norm_w, w_in, conv_w, conv_norm_w, lb_logits, hgrn_norm_w, w_out)


def _kv_kernel(mem_ref, nw_ref, wkv_ref, o_ref):
    mn = _rms(mem_ref[0], nw_ref[...]).astype(BF16)
    o_ref[0] = jnp.dot(mn, wkv_ref[...], preferred_element_type=F32).astype(BF16)


def _kv(mem, mem_norm_w, wkv):
    b, m, d = mem.shape
    return pl.pallas_call(
        _kv_kernel,
        grid=(b,),
        in_specs=[
            pl.BlockSpec((1, m, d), lambda i: (i, 0, 0)),
            pl.BlockSpec((1, d), lambda i: (0, 0)),
            pl.BlockSpec((d, 2 * d), lambda i: (0, 0)),
        ],
        out_specs=pl.BlockSpec((1, m, 2 * d), lambda i: (i, 0, 0)),
        out_shape=jax.ShapeDtypeStruct((b, m, 2 * d), BF16),
        compiler_params=pltpu.CompilerParams(
            dimension_semantics=("arbitrary",), vmem_limit_bytes=V7X_VMEM_LIMIT),
        name="kv",
    )(mem, mem_norm_w, wkv)


def _xattn_kernel(x_ref, nw_ref, wq_ref, kv_ref, wo_ref, o_ref):
    x = x_ref[0]
    d = x.shape[-1]
    hd_dim = d // MEM_HEADS
    h = _rms(x, nw_ref[...]).astype(BF16)
    q = jnp.dot(h, wq_ref[...], preferred_element_type=F32)
    acc = x
    for hd in range(MEM_HEADS):
        cs = slice(hd * hd_dim, (hd + 1) * hd_dim)
        kh = kv_ref[0, :, cs]
        vh = kv_ref[0, :, d + hd * hd_dim:d + (hd + 1) * hd_dim]
        s = lax.dot_general(q[:, cs].astype(BF16), kh, NT_DIMS, preferred_element_type=F32)
        s = s * (hd_dim ** -0.5)
        p = jnp.exp(s - jnp.max(s, axis=-1, keepdims=True))
        p = p / jnp.sum(p, axis=-1, keepdims=True)
        oh = jnp.dot(p.astype(BF16), vh, preferred_element_type=F32)
        acc = acc + jnp.dot(oh.astype(BF16), wo_ref[cs, :], preferred_element_type=F32)
    o_ref[0] = acc


def _xattn(x, xattn_norm_w, wq, kv, wo, *, tm):
    b, t, d = x.shape
    m = kv.shape[1]
    const = lambda *_: (0, 0)
    return pl.pallas_call(
        _xattn_kernel,
        grid=(b, t // tm),
        in_specs=[
            pl.BlockSpec((1, tm, d), lambda i, j: (i, j, 0)),
            pl.BlockSpec((1, d), const),
            pl.BlockSpec((d, d), const),
            pl.BlockSpec((1, m, 2 * d), lambda i, j: (i, 0, 0)),
            pl.BlockSpec((d, d), const),
        ],
        out_specs=pl.BlockSpec((1, tm, d), lambda i, j: (i, j, 0)),
        out_shape=jax.ShapeDtypeStruct((b, t, d), F32),
        compiler_params=pltpu.CompilerParams(
            dimension_semantics=("arbitrary", "arbitrary"), vmem_limit_bytes=V7X_VMEM_LIMIT),
        name="xattn",
    )(x, xattn_norm_w, wq, kv, wo)


def _staircase():
    return [(q, PEER_TOPK // (q + 1)) for q in range(PEER_TOPK)]


N_CAND = sum(n for _, n in _staircase())
N_CAND_PAD = -(-N_CAND // 8) * 8


def _sorting_network(n):
    pairs = []

    def merge(lo, hi, r):
        step = r * 2
        if step < hi - lo:
            merge(lo, hi, step)
            merge(lo + r, hi, step)
            pairs.extend((i, i + r) for i in range(lo + r, hi - r, step))
        else:
            pairs.append((lo, lo + r))

    def sort(lo, hi):
        if hi > lo:
            mid = lo + (hi - lo) // 2
            sort(lo, mid)
            sort(mid + 1, hi)
            merge(lo, hi, 1)

    sort(0, n - 1)
    return pairs


def _top_values_sorted(rows, count, store):
    rows = list(rows)
    while len(rows) & (len(rows) - 1):
        rows.append(jnp.full_like(rows[0], -jnp.inf))
    for i, j in _sorting_network(len(rows)):
        rows[i], rows[j] = jnp.maximum(rows[i], rows[j]), jnp.minimum(rows[i], rows[j])
    m = None
    for r in range(count):
        m = jnp.max(rows[0], axis=0, keepdims=True)
        store(r, m)
        eq = rows[0] == m
        keep = min(len(rows), count - r - 1)
        for q in range(keep):
            nxt = rows[q + 1] if q + 1 < len(rows) else jnp.full_like(rows[q], -jnp.inf)
            rows[q] = jnp.where(eq, nxt, rows[q])
        rows = rows[:keep]
    return m


def _top_values_ranked(s, vals_ref, ls):
    work = s
    rank = jnp.full(s.shape, float(PEER_TOPK), F32)
    for r in range(PEER_TOPK):
        m = jnp.max(work, axis=0, keepdims=True)
        eq = work == m
        rank = jnp.where(eq, float(r), rank)
        work = jnp.where(eq, -jnp.inf, work)
        vals_ref[r:r + 1, ls] = m
    return rank


def _row_blocks(x):
    return [x[i:i + 8, :] for i in range(0, x.shape[0], 8)]


def _select(s1, s2, a_ref, b_ref, cand_ref, ls):
    def store_a(r, m):
        a_ref[r:r + 1, ls] = m

    _top_values_sorted(_row_blocks(s1), PEER_TOPK, store_a)
    rank2 = _top_values_ranked(s2, b_ref, ls)
    a = a_ref[:, ls]
    off = 0
    for qq, n in _staircase():
        cand_ref[off:off + n, ls] = a[0:n, :] + b_ref[qq:qq + 1, ls]
        off += n
    if N_CAND_PAD > N_CAND:
        cand_ref[N_CAND:N_CAND_PAD, ls] = jnp.full((N_CAND_PAD - N_CAND, a.shape[1]), -jnp.inf, F32)
    cand = cand_ref[:, ls]
    tau = _top_values_sorted(_row_blocks(cand), PEER_TOPK, lambda r, m: None)
    z = jnp.sum(jnp.where(cand >= tau, jnp.exp(cand - cand[0:1, :]), 0.0), axis=0, keepdims=True)
    n1 = jnp.zeros(s1.shape, F32)
    for qq in range(PEER_TOPK):
        ok = a + b_ref[qq:qq + 1, ls] >= tau
        theta = jnp.min(jnp.where(ok, a, jnp.inf), axis=0, keepdims=True)
        n1 = jnp.where(s1 >= theta, float(qq + 1), n1)
    g1 = jnp.exp(s1 - a[0:1, :])
    g2 = jnp.exp(s2 - b_ref[0:1, ls]) * (1.0 / z)
    return rank2, n1, g1, g2


def _bf16_bits(x):
    return pltpu.bitcast(x.astype(BF16).astype(F32), jnp.uint32)


def _dup_bf16_words(x):
    bits = _bf16_bits(x)
    return bits | lax.shift_right_logical(bits, jnp.uint32(16))


def _store_row_pairs(dst, x, slab_ref):
    rows, cols = x.shape
    for c in range(cols // V7X_LANES):
        cs = slice(c * V7X_LANES, (c + 1) * V7X_LANES)
        slab_ref[c] = x[:, cs]
        even = slab_ref[c, pl.ds(0, rows // 2, stride=2), :]
        odd = slab_ref[c, pl.ds(1, rows // 2, stride=2), :]
        dst[:, cs] = lax.shift_right_logical(_bf16_bits(even), jnp.uint32(16)) | _bf16_bits(odd)


def _route_kernel(x_ref, nw_ref, wqry_ref, keys_ref, hb_ref, r2_ref, n1_ref, g1_ref, g2_ref,
                  a_ref, b_ref, cand_ref, hslab_ref, kslab_ref, s1_ref, s2_ref):
    hf = _rms(x_ref[...], nw_ref[...])
    tm = hf.shape[0]
    _store_row_pairs(hb_ref, hf, hslab_ref)
    q = jnp.dot(hf.astype(BF16), wqry_ref[...], preferred_element_type=F32).astype(BF16)
    for hd in range(PEER_HEADS):
        c1 = (2 * hd) * PEER_DK
        c2 = (2 * hd + 1) * PEER_DK
        s1_ref[...] = lax.dot_general(keys_ref[2 * hd], q[:, c1:c1 + PEER_DK], NT_DIMS,
                                      preferred_element_type=F32)
        s2_ref[...] = lax.dot_general(keys_ref[2 * hd + 1], q[:, c2:c2 + PEER_DK], NT_DIMS,
                                      preferred_element_type=F32)
        for lt in range(tm // V7X_LANES):
            ls = slice(lt * V7X_LANES, (lt + 1) * V7X_LANES)
            rank2, n1, g1, g2 = _select(s1_ref[:, ls], s2_ref[:, ls], a_ref, b_ref, cand_ref, ls)
            _store_row_pairs(r2_ref.at[hd, :, ls], rank2, kslab_ref.at[pl.ds(2 * lt, 1)])
            n1_ref[hd, :, ls] = _dup_bf16_words(n1)
            g1_ref[hd, :, ls] = _dup_bf16_words(g1)
            _store_row_pairs(g2_ref.at[hd, :, ls], g2, kslab_ref.at[pl.ds(2 * lt + 1, 1)])


def _route(x2, ffn_norm_w, w_query, keys, *, tm):
    n, d = x2.shape
    nq = w_query.shape[1]
    tab = jax.ShapeDtypeStruct((PEER_HEADS, PEER_N_KEYS, n), jnp.uint32)
    tabh = jax.ShapeDtypeStruct((PEER_HEADS, PEER_N_KEYS // 2, n), jnp.uint32)
    tab_spec = pl.BlockSpec((PEER_HEADS, PEER_N_KEYS, tm), lambda i: (0, 0, i))
    tabh_spec = pl.BlockSpec((PEER_HEADS, PEER_N_KEYS // 2, tm), lambda i: (0, 0, i))
    return pl.pallas_call(
        _route_kernel,
        grid=(n // tm,),
        in_specs=[
            pl.BlockSpec((tm, d), lambda i: (i, 0)),
            pl.BlockSpec((1, d), lambda i: (0, 0)),
            pl.BlockSpec((d, nq), lambda i: (0, 0)),
            pl.BlockSpec(keys.shape, lambda i: (0, 0, 0)),
        ],
        out_specs=[pl.BlockSpec((tm // 2, d), lambda i: (i, 0)), tabh_spec, tab_spec, tab_spec, tabh_spec],
        out_shape=[jax.ShapeDtypeStruct((n // 2, d), jnp.uint32), tabh, tab, tab, tabh],
        scratch_shapes=[
            pltpu.VMEM((PEER_TOPK, tm), F32),
            pltpu.VMEM((PEER_TOPK, tm), F32),
            pltpu.VMEM((N_CAND_PAD, tm), F32),
            pltpu.VMEM((d // V7X_LANES, tm, V7X_LANES), F32),
            pltpu.VMEM((2 * tm // V7X_LANES, PEER_N_KEYS, V7X_LANES), F32),
            pltpu.VMEM((PEER_N_KEYS, tm), F32),
            pltpu.VMEM((PEER_N_KEYS, tm), F32),
        ],
        compiler_params=pltpu.CompilerParams(
            dimension_semantics=("arbitrary",), vmem_limit_bytes=V7X_VMEM_LIMIT),
        name="route",
    )(x2, ffn_norm_w, w_query, keys)


def _tables_kernel(down_ref, up_ref, u_ref, vt_ref, uslab_ref, vslab_ref):
    _store_row_pairs(u_ref, down_ref[...], uslab_ref)
    _store_row_pairs(vt_ref, up_ref[...].T, vslab_ref)


def _tables(peer_down, peer_up, *, te):
    ne, d = peer_down.shape
    return pl.pallas_call(
        _tables_kernel,
        grid=(ne // te,),
        in_specs=[pl.BlockSpec((te, d), lambda i: (i, 0)), pl.BlockSpec((te, d), lambda i: (i, 0))],
        out_specs=[pl.BlockSpec((te // 2, d), lambda i: (i, 0)), pl.BlockSpec((d // 2, te), lambda i: (0, i))],
        out_shape=[jax.ShapeDtypeStruct((ne // 2, d), jnp.uint32), jax.ShapeDtypeStruct((d // 2, ne), jnp.uint32)],
        scratch_shapes=[pltpu.VMEM((d // V7X_LANES, te, V7X_LANES), F32),
                        pltpu.VMEM((te // V7X_LANES, d, V7X_LANES), F32)],
        compiler_params=pltpu.CompilerParams(
            dimension_semantics=("arbitrary",), vmem_limit_bytes=V7X_VMEM_LIMIT),
        name="tables",
    )(peer_down, peer_up)


def _peer_kernel(hb_ref, u_ref, vt_ref, r2_ref, n1_ref, g1_ref, g2_ref, x2_ref, fnw_ref, o_ref,
                 s0_ref, s1_ref, h0_ref, h1_ref, acc_ref, *, tt, te, n_etiles):
    k = pl.program_id(0)
    kc = k - 2
    nk = PEER_N_KEYS
    lw = 2 * V7X_LANES

    @pl.when(k == 0)
    def _():
        for ref in (s0_ref, s1_ref, h0_ref, h1_ref):
            ref[...] = jnp.zeros_like(ref)

    @pl.when((k == 0) | ((kc >= 0) & (lax.rem(kc, n_etiles) == 0)))
    def _():
        acc_ref[...] = jnp.zeros_like(acc_ref)

    mc = 2 * nk
    d = acc_ref.shape[0]
    pieces_a = [(m, l) for l in range(tt // lw) for m in range(te // mc)]
    pieces_c = [(m, l) for l in range(tt // lw) for m in range(d // mc)]
    nb_rows = nk // 2
    ii_group = 2
    blocks_b = [(ig, jh, l) for l in range(tt // lw) for ig in range(te // nk // ii_group)
                for jh in range(nk // nb_rows)]

    def words(ref, *idx):
        return pltpu.bitcast(ref[idx], BF16)

    def stages(sa_ref, sb_ref, hb_out_ref, hc_ref):
        def stage_a(m, l):
            rs, ls = slice(m * mc, (m + 1) * mc), slice(l * lw, (l + 1) * lw)
            u = words(u_ref, slice(m * mc // 2, (m + 1) * mc // 2), slice(None))
            hb = words(hb_ref, slice(l * lw // 2, (l + 1) * lw // 2), slice(None))
            sa_ref[rs, ls] = lax.dot_general(u, hb, NT_DIMS, preferred_element_type=F32)

        def stage_b(ig, jh, l):
            ls = slice(l * lw, (l + 1) * lw)
            js = slice(jh * nb_rows // 2, (jh + 1) * nb_rows // 2)
            iis = range(ig * ii_group, (ig + 1) * ii_group)
            w = [jnp.zeros((nb_rows, lw), BF16) for _ in iis]
            for hd in range(PEER_HEADS):
                g2 = words(g2_ref, hd, js, ls)
                r2 = words(r2_ref, hd, js, ls)
                for t, ii in enumerate(iis):
                    n1 = pltpu.bitcast(jnp.broadcast_to(n1_ref[hd, ii:ii + 1, ls], (nb_rows // 2, lw)), BF16)
                    g1 = pltpu.bitcast(jnp.broadcast_to(g1_ref[hd, ii:ii + 1, ls], (nb_rows // 2, lw)), BF16)
                    w[t] = w[t] + jnp.where(r2 < n1, g2 * g1, jnp.zeros_like(g2))
            for t, ii in enumerate(iis):
                rs = slice(ii * nk + jh * nb_rows, ii * nk + (jh + 1) * nb_rows)
                s = sb_ref[rs, ls]
                act = 0.5 * s * (1.0 + lax.erf(s * 0.7071067811865476))
                hb_out_ref[rs, ls] = w[t] * act.astype(BF16)

        def stage_c(m, l):
            rs, ls = slice(m * mc, (m + 1) * mc), slice(l * lw, (l + 1) * lw)
            vt = words(vt_ref, slice(m * mc // 2, (m + 1) * mc // 2), slice(None))
            acc_ref[rs, ls] += jnp.dot(vt, hc_ref[:, ls], preferred_element_type=F32)

        nb = len(blocks_b) // len(pieces_a)
        for p in range(len(pieces_a)):
            for q in range(nb):
                stage_b(*blocks_b[p * nb + q])
            stage_a(*pieces_a[p])
            stage_c(*pieces_c[p])

    @pl.when(lax.rem(k, 2) == 0)
    def _():
        stages(s0_ref, s1_ref, h1_ref, h0_ref)

    @pl.when(lax.rem(k, 2) == 1)
    def _():
        stages(s1_ref, s0_ref, h0_ref, h1_ref)

    @pl.when((kc >= 0) & (lax.rem(kc, n_etiles) == n_etiles - 1))
    def _():
        z = x2_ref[...] + acc_ref[...].T
        o_ref[...] = _rms(z, fnw_ref[...])


def _peer(hb, u, vt, r2, n1, g1, g2, x2, final_norm_w, *, tt, te):
    n, d = x2.shape
    ne = vt.shape[1]
    n_ttiles, n_etiles = n // tt, ne // te
    n_tiles = n_ttiles * n_etiles
    ni = te // PEER_N_KEYS
    assert ni == 8, "the first-half key rows of one expert tile must be one 32-bit sublane group"

    def tile(k, lag):
        t = jnp.clip(k - lag, 0, n_tiles - 1)
        return t // n_etiles, lax.rem(t, n_etiles)

    tab_spec = pl.BlockSpec((PEER_HEADS, PEER_N_KEYS // 2, tt), lambda k: (0, 0, tile(k, 1)[0]))
    row_spec = pl.BlockSpec((PEER_HEADS, ni, tt), lambda k: (0, tile(k, 1)[1], tile(k, 1)[0]))
    return pl.pallas_call(
        functools.partial(_peer_kernel, tt=tt, te=te, n_etiles=n_etiles),
        grid=(n_tiles + 2,),
        in_specs=[
            pl.BlockSpec((tt // 2, d), lambda k: (tile(k, 0)[0], 0)),
            pl.BlockSpec((te // 2, d), lambda k: (tile(k, 0)[1], 0)),
            pl.BlockSpec((d // 2, te), lambda k: (0, tile(k, 2)[1])),
            tab_spec, row_spec, row_spec, tab_spec,
            pl.BlockSpec((tt, d), lambda k: (tile(k, 2)[0], 0)),
            pl.BlockSpec((1, d), lambda k: (0, 0)),
        ],
        out_specs=pl.BlockSpec((tt, d), lambda k: (tile(k, 2)[0], 0)),
        out_shape=jax.ShapeDtypeStruct((n, d), F32),
        scratch_shapes=[
            pltpu.VMEM((te, tt), F32),
            pltpu.VMEM((te, tt), F32),
            pltpu.VMEM((te, tt), BF16),
            pltpu.VMEM((te, tt), BF16),
            pltpu.VMEM((d, tt), F32),
        ],
        compiler_params=pltpu.CompilerParams(
            dimension_semantics=("arbitrary",), vmem_limit_bytes=V7X_VMEM_LIMIT),
        name="peer",
    )(hb, u, vt, r2, n1, g1, g2, x2, final_norm_w)


def kernel(x, mem, mix_norm_w, w_in, conv_w, conv_norm_w, hgrn_lb_logits, hgrn_norm_w, w_out, xattn_norm_w,
           mem_norm_w, wq_mem, wkv_mem, wo_mem, ffn_norm_w, peer_w_query, peer_sub_keys, peer_down, peer_up,
           final_norm_w):
    b, t, d = x.shape
    depth = w_in.shape[0]
    assert depth == 1, "single-layer problem"
    assert t % 512 == 0 and d % V7X_LANES == 0
    row = lambda v: v.reshape(1, -1)
    x1 = _mixer(x, row(mix_norm_w[0]), w_in[0].astype(BF16), conv_w[0], row(conv_norm_w[0]), hgrn_lb_logits,
                row(hgrn_norm_w[0]), w_out[0].astype(BF16), tm=512)
    kv = _kv(mem, row(mem_norm_w[0]), wkv_mem[0].astype(BF16))
    x2 = _xattn(x1, row(xattn_norm_w[0]), wq_mem[0].astype(BF16), kv, wo_mem[0].astype(BF16), tm=512)
    x2 = x2.reshape(b * t, d)
    keys = peer_sub_keys[0].reshape(2 * PEER_HEADS, PEER_N_KEYS, PEER_DK).astype(BF16)
    hb, r2, n1, g1, g2 = _route(x2, row(ffn_norm_w[0]), peer_w_query[0].astype(BF16), keys, tm=512)
    u, vt = _tables(peer_down[0], peer_up[0], te=512)
    out = _peer(hb, u, vt, r2, n1, g1, g2, x2, row(final_norm_w), tt=512, te=1024)
    return out.reshape(b, t, d)
```

```python
import functools

import jax
import jax.numpy as jnp
from jax import lax
from jax.experimental import pallas as pl
from jax.experimental.pallas import tpu as pltpu

EPS = 1e-6
CONV_WIDTH = 512
HGRN_HEADS = 4
HGRN_HEAD_DIM = 128
HGRN_WIDTH = HGRN_HEADS * HGRN_HEAD_DIM
CHUNK = 64
MEM_HEADS = 4
PEER_HEADS = 8
PEER_N_KEYS = 128
PEER_DK = 128
PEER_TOPK = 16
V7X_LANES = 128
BF16_SUBLANES = 16
V7X_VMEM_LIMIT = 56 * 1024 * 1024

BF16 = jnp.bfloat16
F32 = jnp.float32
NT_DIMS = (((1,), (1,)), ((), ()))
TN_DIMS = (((0,), (0,)), ((), ()))


def _rms(x, w):
    return x * lax.rsqrt(jnp.mean(x * x, axis=-1, keepdims=True) + EPS) * w


def _sigmoid(x):
    return 1.0 / (1.0 + jnp.exp(-x))


def _mixer_kernel(x_ref, nw_ref, win_ref, convw_ref, cnw_ref, lbl_ref, hnw_ref, wout_ref, o_ref,
                  proj_ref, ohg_ref, st_ref, halo_ref, *, tm):
    cw, hw, dh = CONV_WIDTH, HGRN_WIDTH, HGRN_HEAD_DIM

    @pl.when(pl.program_id(1) == 0)
    def _():
        st_ref[...] = jnp.zeros_like(st_ref)
        halo_ref[...] = jnp.zeros_like(halo_ref)

    x = x_ref[0]
    h = _rms(x, nw_ref[...]).astype(BF16)
    proj_ref[...] = jnp.dot(h, win_ref[...], preferred_element_type=F32)

    bg = proj_ref[:, 0:cw]
    u = proj_ref[:, cw:2 * cw] * proj_ref[:, 2 * cw:3 * cw]
    row = lax.broadcasted_iota(jnp.int32, (tm, cw), 0)
    prev = halo_ref[...]
    u1 = jnp.where(row == 0, prev[7:8, :], pltpu.roll(u, 1, 0))
    u2 = jnp.where(row == 0, prev[6:7, :], jnp.where(row == 1, prev[7:8, :], pltpu.roll(u, 2, 0)))
    halo_ref[...] = u[tm - 8:tm, :]
    cwt = convw_ref[...]
    yc = _rms(bg * (u2 * cwt[0:1, :] + u1 * cwt[1:2, :] + u * cwt[2:3, :]), cnw_ref[...])

    lbl = lbl_ref[...]
    lmax = jnp.max(lbl, axis=0, keepdims=True)
    lexp = jnp.exp(lbl - lmax)
    lb = lexp[0:1, :] / jnp.sum(lexp, axis=0, keepdims=True)
    o0 = 3 * cw
    rowc = lax.broadcasted_iota(jnp.int32, (CHUNK, hw), 0)
    tril = (lax.broadcasted_iota(jnp.int32, (CHUNK, CHUNK), 0)
            >= lax.broadcasted_iota(jnp.int32, (CHUNK, CHUNK), 1))
    for c in range(tm // CHUNK):
        r0 = c * CHUNK
        qc = proj_ref[r0:r0 + CHUNK, o0:o0 + hw]
        fc = proj_ref[r0:r0 + CHUNK, o0 + hw:o0 + 2 * hw]
        ic = proj_ref[r0:r0 + CHUNK, o0 + 2 * hw:o0 + 3 * hw]
        qa = qc * _sigmoid(qc)
        fg = lb + (1.0 - lb) * _sigmoid(fc)
        kk = 1.0 - fg
        g = jnp.log(fg)
        for s in (1, 2, 4, 8, 16, 32):
            g = g + jnp.where(rowc >= s, pltpu.roll(g, s, 0), 0.0)
        gl = g[CHUNK - 1:CHUNK, :]
        q_dec = (qa * jnp.exp(g)).astype(BF16)
        k_inv = (kk * jnp.exp(-g)).astype(BF16)
        k_end = (kk * jnp.exp(gl - g)).astype(BF16)
        dec = jnp.exp(gl)
        vb = ic.astype(BF16)
        for hd in range(HGRN_HEADS):
            cs = slice(hd * dh, (hd + 1) * dh)
            a = lax.dot_general(q_dec[:, cs], k_inv[:, cs], NT_DIMS, preferred_element_type=F32)
            a = jnp.where(tril, a, 0.0).astype(BF16)
            st = st_ref[hd]
            o = (jnp.dot(a, vb[:, cs], preferred_element_type=F32)
                 + lax.dot_general(q_dec[:, cs], st.astype(BF16), NT_DIMS, preferred_element_type=F32))
            ohg_ref[r0:r0 + CHUNK, cs] = o
            ut = lax.dot_general(vb[:, cs], k_end[:, cs], TN_DIMS, preferred_element_type=F32)
            st_ref[hd] = st * dec[:, cs] + ut

    gate = proj_ref[:, o0 + 3 * hw:o0 + 4 * hw]
    gate = gate * _sigmoid(gate)
    hnw = hnw_ref[...]
    acc = x + jnp.dot(yc.astype(BF16), wout_ref[0:cw, :], preferred_element_type=F32)
    for hd in range(HGRN_HEADS):
        cs = slice(hd * dh, (hd + 1) * dh)
        oh = ohg_ref[:, cs]
        oh = oh * lax.rsqrt(jnp.mean(oh * oh, axis=-1, keepdims=True) + EPS)
        yh = oh * hnw[:, cs] * gate[:, cs]
        acc = acc + jnp.dot(yh.astype(BF16), wout_ref[cw + hd * dh:cw + (hd + 1) * dh, :],
                            preferred_element_type=F32)
    o_ref[0] = acc


def _mixer(x, mix_norm_w, w_in, conv_w, conv_norm_w, lb_logits, hgrn_norm_w, w_out, *, tm):
    b, t, d = x.shape
    ncols = w_in.shape[1]
    const = lambda *_: (0, 0)
    return pl.pallas_call(
        functools.partial(_mixer_kernel, tm=tm),
        grid=(b, t // tm),
        in_specs=[
            pl.BlockSpec((1, tm, d), lambda i, j: (i, j, 0)),
            pl.BlockSpec((1, d), const),
            pl.BlockSpec((d, ncols), const),
            pl.BlockSpec(conv_w.shape, const),
            pl.BlockSpec((1, CONV_WIDTH), const),
            pl.BlockSpec(lb_logits.shape, const),
            pl.BlockSpec((1, HGRN_WIDTH), const),
            pl.BlockSpec((d, d), const),
        ],
        out_specs=pl.BlockSpec((1, tm, d), lambda i, j: (i, j, 0)),
        out_shape=jax.ShapeDtypeStruct((b, t, d), F32),
        scratch_shapes=[
            pltpu.VMEM((tm, ncols), F32),
            pltpu.VMEM((tm, HGRN_WIDTH), F32),
            pltpu.VMEM((HGRN_HEADS, HGRN_HEAD_DIM, HGRN_HEAD_DIM), F32),
            pltpu.VMEM((8, CONV_WIDTH), F32),
        ],
        compiler_params=pltpu.CompilerParams(
            dimension_semantics=("arbitrary", "arbitrary"), vmem_limit_bytes=V7X_VMEM_LIMIT),
        name="mixer",
    )(x, mix_norm_w, w_in, conv_w, conv_norm_w, lb_logits, hgrn_norm_w, w_out)


def _kv_kernel(mem_ref, nw_ref, wkv_ref, o_ref):
    mn = _rms(mem_ref[0], nw_ref[...]).astype(BF16)
    o_ref[0] = jnp.dot(mn, wkv_ref[...], preferred_element_type=F32).astype(BF16)


def _kv(mem, mem_norm_w, wkv):
    b, m, d = mem.shape
    return pl.pallas_call(
        _kv_kernel,
        grid=(b,),
        in_specs=[
            pl.BlockSpec((1, m, d), lambda i: (i, 0, 0)),
            pl.BlockSpec((1, d), lambda i: (0, 0)),
            pl.BlockSpec((d, 2 * d), lambda i: (0, 0)),
        ],
        out_specs=pl.BlockSpec((1, m, 2 * d), lambda i: (i, 0, 0)),
        out_shape=jax.ShapeDtypeStruct((b, m, 2 * d), BF16),
        compiler_params=pltpu.CompilerParams(
            dimension_semantics=("arbitrary",), vmem_limit_bytes=V7X_VMEM_LIMIT),
        name="kv",
    )(mem, mem_norm_w, wkv)


def _xattn_kernel(x_ref, nw_ref, wq_ref, kv_ref, wo_ref, o_ref):
    x = x_ref[0]
    d = x.shape[-1]
    hd_dim = d // MEM_HEADS
    h = _rms(x, nw_ref[...]).astype(BF16)
    q = jnp.dot(h, wq_ref[...], preferred_element_type=F32)
    acc = x
    for hd in range(MEM_HEADS):
        cs = slice(hd * hd_dim, (hd + 1) * hd_dim)
        kh = kv_ref[0, :, cs]
        vh = kv_ref[0, :, d + hd * hd_dim:d + (hd + 1) * hd_dim]
        s = lax.dot_general(q[:, cs].astype(BF16), kh, NT_DIMS, preferred_element_type=F32)
        s = s * (hd_dim ** -0.5)
        p = jnp.exp(s - jnp.max(s, axis=-1, keepdims=True))
        p = p / jnp.sum(p, axis=-1, keepdims=True)
        oh = jnp.dot(p.astype(BF16), vh, preferred_element_type=F32)
        acc = acc + jnp.dot(oh.astype(BF16), wo_ref[cs, :], preferred_element_type=F32)
    o_ref[0] = acc


def _xattn(x, xattn_norm_w, wq, kv, wo, *, tm):
    b, t, d = x.shape
    m = kv.shape[1]
    const = lambda *_: (0, 0)
    return pl.pallas_call(
        _xattn_kernel,
        grid=(b, t // tm),
        in_specs=[
            pl.BlockSpec((1, tm, d), lambda i, j: (i, j, 0)),
            pl.BlockSpec((1, d), const),
            pl.BlockSpec((d, d), const),
            pl.BlockSpec((1, m, 2 * d), lambda i, j: (i, 0, 0)),
            pl.BlockSpec((d, d), const),
        ],
        out_specs=pl.BlockSpec((1, tm, d), lambda i, j: (i, j, 0)),
        out_shape=jax.ShapeDtypeStruct((b, t, d), F32),
        compiler_params=pltpu.CompilerParams(
            dimension_semantics=("arbitrary", "arbitrary"), vmem_limit_bytes=V7X_VMEM_LIMIT),
        name="xattn",
    )(x, xattn_norm_w, wq, kv, wo)


def _staircase():
    return [(q, PEER_TOPK // (q + 1)) for q in range(PEER_TOPK)]


N_CAND = sum(n for _, n in _staircase())
N_CAND_PAD = -(-N_CAND // 8) * 8


def _sorting_network(n):
    pairs = []

    def merge(lo, hi, r):
        step = r * 2
        if step < hi - lo:
            merge(lo, hi, step)
            merge(lo + r, hi, step)
            pairs.extend((i, i + r) for i in range(lo + r, hi - r, step))
        else:
            pairs.append((lo, lo + r))

    def sort(lo, hi):
        if hi > lo:
            mid = lo + (hi - lo) // 2
            sort(lo, mid)
            sort(mid + 1, hi)
            merge(lo, hi, 1)

    sort(0, n - 1)
    return pairs


def _top_values_sorted(rows, count, store):
    rows = list(rows)
    while len(rows) & (len(rows) - 1):
        rows.append(jnp.full_like(rows[0], -jnp.inf))
    for i, j in _sorting_network(len(rows)):
        rows[i], rows[j] = jnp.maximum(rows[i], rows[j]), jnp.minimum(rows[i], rows[j])
    m = None
    for r in range(count):
        m = jnp.max(rows[0], axis=0, keepdims=True)
        store(r, m)
        eq = rows[0] == m
        keep = min(len(rows), count - r - 1)
        for q in range(keep):
            nxt = rows[q + 1] if q + 1 < len(rows) else jnp.full_like(rows[q], -jnp.inf)
            rows[q] = jnp.where(eq, nxt, rows[q])
        rows = rows[:keep]
    return m


def _row_blocks(x):
    return [x[i:i + 8, :] for i in range(0, x.shape[0], 8)]


def _select(s1, s2, a_ref, b_ref, cand_ref, ls):
    def store_a(r, m):
        a_ref[r:r + 1, ls] = m

    def store_b(r, m):
        b_ref[r:r + 1, ls] = m

    _top_values_sorted(_row_blocks(s1), PEER_TOPK, store_a)
    _top_values_sorted(_row_blocks(s2), PEER_TOPK, store_b)
    rank2 = jnp.zeros(s2.shape, F32)
    for r in range(PEER_TOPK):
        rank2 = jnp.where(s2 < b_ref[r:r + 1, ls], float(r + 1), rank2)
    a = a_ref[:, ls]
    off = 0
    for qq, n in _staircase():
        cand_ref[off:off + n, ls] = a[0:n, :] + b_ref[qq:qq + 1, ls]
        off += n
    if N_CAND_PAD > N_CAND:
        cand_ref[N_CAND:N_CAND_PAD, ls] = jnp.full((N_CAND_PAD - N_CAND, a.shape[1]), -jnp.inf, F32)
    cand = cand_ref[:, ls]
    tau = _top_values_sorted(_row_blocks(cand), PEER_TOPK, lambda r, m: None)
    z = jnp.sum(jnp.where(cand >= tau, jnp.exp(cand - cand[0:1, :]), 0.0), axis=0, keepdims=True)
    n1 = jnp.zeros(s1.shape, F32)
    for qq in range(PEER_TOPK):
        ok = a + b_ref[qq:qq + 1, ls] >= tau
        theta = jnp.min(jnp.where(ok, a, jnp.inf), axis=0, keepdims=True)
        n1 = jnp.where(s1 >= theta, float(qq + 1), n1)
    g1 = jnp.exp(s1 - a[0:1, :])
    g2 = jnp.exp(s2 - b_ref[0:1, ls]) * (1.0 / z)
    return rank2, n1, g1, g2


def _bf16_bits(x):
    return pltpu.bitcast(x.astype(BF16).astype(F32), jnp.uint32)


def _dup_bf16_words(x):
    bits = _bf16_bits(x)
    return bits | lax.shift_right_logical(bits, jnp.uint32(16))


def _store_row_pairs(dst, x, slab_ref):
    rows, cols = x.shape
    for c in range(cols // V7X_LANES):
        cs = slice(c * V7X_LANES, (c + 1) * V7X_LANES)
        slab_ref[c] = x[:, cs]
        even = slab_ref[c, pl.ds(0, rows // 2, stride=2), :]
        odd = slab_ref[c, pl.ds(1, rows // 2, stride=2), :]
        dst[:, cs] = lax.shift_right_logical(_bf16_bits(even), jnp.uint32(16)) | _bf16_bits(odd)


def _route_kernel(x_ref, nw_ref, wqry_ref, keys_ref, hb_ref, r2_ref, n1_ref, g1_ref, g2_ref,
                  a_ref, b_ref, cand_ref, hslab_ref, kslab_ref, s1_ref, s2_ref):
    hf = _rms(x_ref[...], nw_ref[...])
    tm = hf.shape[0]
    _store_row_pairs(hb_ref, hf, hslab_ref)
    q = jnp.dot(hf.astype(BF16), wqry_ref[...], preferred_element_type=F32).astype(BF16)
    for hd in range(PEER_HEADS):
        c1 = (2 * hd) * PEER_DK
        c2 = (2 * hd + 1) * PEER_DK
        s1_ref[...] = lax.dot_general(keys_ref[2 * hd], q[:, c1:c1 + PEER_DK], NT_DIMS,
                                      preferred_element_type=F32)
        s2_ref[...] = lax.dot_general(keys_ref[2 * hd + 1], q[:, c2:c2 + PEER_DK], NT_DIMS,
                                      preferred_element_type=F32)
        for lt in range(tm // V7X_LANES):
            ls = slice(lt * V7X_LANES, (lt + 1) * V7X_LANES)
            rank2, n1, g1, g2 = _select(s1_ref[:, ls], s2_ref[:, ls], a_ref, b_ref, cand_ref, ls)
            _store_row_pairs(r2_ref.at[hd, :, ls], rank2, kslab_ref.at[pl.ds(2 * lt, 1)])
            n1_ref[hd, :, ls] = _dup_bf16_words(n1)
            g1_ref[hd, :, ls] = _dup_bf16_words(g1)
            _store_row_pairs(g2_ref.at[hd, :, ls], g2, kslab_ref.at[pl.ds(2 * lt + 1, 1)])


def _route(x2, ffn_norm_w, w_query, keys, *, tm):
    n, d = x2.shape
    nq = w_query.shape[1]
    tab = jax.ShapeDtypeStruct((PEER_HEADS, PEER_N_KEYS, n), jnp.uint32)
    tabh = jax.ShapeDtypeStruct((PEER_HEADS, PEER_N_KEYS // 2, n), jnp.uint32)
    tab_spec = pl.BlockSpec((PEER_HEADS, PEER_N_KEYS, tm), lambda i: (0, 0, i))
    tabh_spec = pl.BlockSpec((PEER_HEADS, PEER_N_KEYS // 2, tm), lambda i: (0, 0, i))
    return pl.pallas_call(
        _route_kernel,
        grid=(n // tm,),
        in_specs=[
            pl.BlockSpec((tm, d), lambda i: (i, 0)),
            pl.BlockSpec((1, d), lambda i: (0, 0)),
            pl.BlockSpec((d, nq), lambda i: (0, 0)),
            pl.BlockSpec(keys.shape, lambda i: (0, 0, 0)),
        ],
        out_specs=[pl.BlockSpec((tm // 2, d), lambda i: (i, 0)), tabh_spec, tab_spec, tab_spec, tabh_spec],
        out_shape=[jax.ShapeDtypeStruct((n // 2, d), jnp.uint32), tabh, tab, tab, tabh],
        scratch_shapes=[
            pltpu.VMEM((PEER_TOPK, tm), F32),
            pltpu.VMEM((PEER_TOPK, tm), F32),
            pltpu.VMEM((N_CAND_PAD, tm), F32),
            pltpu.VMEM((d // V7X_LANES, tm, V7X_LANES), F32),
            pltpu.VMEM((2 * tm // V7X_LANES, PEER_N_KEYS, V7X_LANES), F32),
            pltpu.VMEM((PEER_N_KEYS, tm), F32),
            pltpu.VMEM((PEER_N_KEYS, tm), F32),
        ],
        compiler_params=pltpu.CompilerParams(
            dimension_semantics=("arbitrary",), vmem_limit_bytes=V7X_VMEM_LIMIT),
        name="route",
    )(x2, ffn_norm_w, w_query, keys)


def _tables_kernel(down_ref, up_ref, u_ref, vt_ref, uslab_ref, vslab_ref):
    _store_row_pairs(u_ref, down_ref[...], uslab_ref)
    _store_row_pairs(vt_ref, up_ref[...].T, vslab_ref)


def _tables(peer_down, peer_up, *, te):
    ne, d = peer_down.shape
    return pl.pallas_call(
        _tables_kernel,
        grid=(ne // te,),
        in_specs=[pl.BlockSpec((te, d), lambda i: (i, 0)), pl.BlockSpec((te, d), lambda i: (i, 0))],
        out_specs=[pl.BlockSpec((te // 2, d), lambda i: (i, 0)), pl.BlockSpec((d // 2, te), lambda i: (0, i))],
        out_shape=[jax.ShapeDtypeStruct((ne // 2, d), jnp.uint32), jax.ShapeDtypeStruct((d // 2, ne), jnp.uint32)],
        scratch_shapes=[pltpu.VMEM((d // V7X_LANES, te, V7X_LANES), F32),
                        pltpu.VMEM((te // V7X_LANES, d, V7X_LANES), F32)],
        compiler_params=pltpu.CompilerParams(
            dimension_semantics=("arbitrary",), vmem_limit_bytes=V7X_VMEM_LIMIT),
        name="tables",
    )(peer_down, peer_up)


def _divmod_nonneg(t, n):
    if n & (n - 1) == 0:
        return lax.shift_right_logical(t, n.bit_length() - 1), t & (n - 1)
    return t // n, lax.rem(t, n)


def _peer_kernel(hb_ref, u_ref, vt_ref, r2_ref, n1_ref, g1_ref, g2_ref, x2_ref, fnw_ref, o_ref,
                 s0_ref, s1_ref, h0_ref, h1_ref, acc_ref, *, tt, te, n_etiles):
    k = pl.program_id(0)
    kc = k - 2
    ec = _divmod_nonneg(jnp.maximum(kc, 0), n_etiles)[1]
    nk = PEER_N_KEYS
    lw = 2 * V7X_LANES

    @pl.when(k == 0)
    def _():
        for ref in (s0_ref, s1_ref, h0_ref, h1_ref):
            ref[...] = jnp.zeros_like(ref)

    @pl.when((k == 0) | ((kc >= 0) & (ec == 0)))
    def _():
        acc_ref[...] = jnp.zeros_like(acc_ref)

    mc = 2 * nk
    d = acc_ref.shape[0]
    pieces_a = [(m, l) for l in range(tt // lw) for m in range(te // mc)]
    pieces_c = [(m, l) for l in range(tt // lw) for m in range(d // mc)]
    nb_rows = nk // 2
    ii_group = 2
    blocks_b = [(ig, jh, l) for l in range(tt // lw) for ig in range(te // nk // ii_group)
                for jh in range(nk // nb_rows)]

    def words(ref, *idx):
        return pltpu.bitcast(ref[idx], BF16)

    def stages(sa_ref, sb_ref, hb_out_ref, hc_ref):
        def stage_a(m, l):
            rs, ls = slice(m * mc, (m + 1) * mc), slice(l * lw, (l + 1) * lw)
            u = words(u_ref, slice(m * mc // 2, (m + 1) * mc // 2), slice(None))
            hb = words(hb_ref, slice(l * lw // 2, (l + 1) * lw // 2), slice(None))
            sa_ref[rs, ls] = lax.dot_general(u, hb, NT_DIMS, preferred_element_type=F32)

        def stage_b(ig, jh, l):
            ls = slice(l * lw, (l + 1) * lw)
            js = slice(jh * nb_rows // 2, (jh + 1) * nb_rows // 2)
            iis = range(ig * ii_group, (ig + 1) * ii_group)
            w = [jnp.zeros((nb_rows, lw), BF16) for _ in iis]
            for hd in range(PEER_HEADS):
                g2 = words(g2_ref, hd, js, ls)
                r2 = words(r2_ref, hd, js, ls)
                for t, ii in enumerate(iis):
                    n1 = pltpu.bitcast(jnp.broadcast_to(n1_ref[hd, ii:ii + 1, ls], (nb_rows // 2, lw)), BF16)
                    g1 = pltpu.bitcast(jnp.broadcast_to(g1_ref[hd, ii:ii + 1, ls], (nb_rows // 2, lw)), BF16)
                    w[t] = w[t] + jnp.where(r2 < n1, g2 * g1, jnp.zeros_like(g2))
            for t, ii in enumerate(iis):
                rs = slice(ii * nk + jh * nb_rows, ii * nk + (jh + 1) * nb_rows)
                s = sb_ref[rs, ls]
                act = 0.5 * s * (1.0 + lax.erf(s * 0.7071067811865476))
                hb_out_ref[rs, ls] = w[t] * act.astype(BF16)

        def stage_c(m, l):
            rs, ls = slice(m * mc, (m + 1) * mc), slice(l * lw, (l + 1) * lw)
            vt = words(vt_ref, slice(m * mc // 2, (m + 1) * mc // 2), slice(None))
            acc_ref[rs, ls] += jnp.dot(vt, hc_ref[:, ls], preferred_element_type=F32)

        nb = len(blocks_b) // len(pieces_a)
        for p in range(len(pieces_a)):
            for q in range(nb):
                stage_b(*blocks_b[p * nb + q])
            stage_a(*pieces_a[p])
            stage_c(*pieces_c[p])

    @pl.when(lax.rem(k, 2) == 0)
    def _():
        stages(s0_ref, s1_ref, h1_ref, h0_ref)

    @pl.when(lax.rem(k, 2) == 1)
    def _():
        stages(s1_ref, s0_ref, h0_ref, h1_ref)

    @pl.when((kc >= 0) & (ec == n_etiles - 1))
    def _():
        z = x2_ref[...] + acc_ref[...].T
        o_ref[...] = _rms(z, fnw_ref[...])


def _peer(hb, u, vt, r2, n1, g1, g2, x2, final_norm_w, *, tt, te):
    n, d = x2.shape
    ne = vt.shape[1]
    n_ttiles, n_etiles = n // tt, ne // te
    n_tiles = n_ttiles * n_etiles
    ni = te // PEER_N_KEYS
    assert ni == 8, "the first-half key rows of one expert tile must be one 32-bit sublane group"

    def tile(k, lag):
        return _divmod_nonneg(jnp.clip(k - lag, 0, n_tiles - 1), n_etiles)

    tab_spec = pl.BlockSpec((PEER_HEADS, PEER_N_KEYS // 2, tt), lambda k: (0, 0, tile(k, 1)[0]))
    row_spec = pl.BlockSpec((PEER_HEADS, ni, tt), lambda k: (0, tile(k, 1)[1], tile(k, 1)[0]))
    return pl.pallas_call(
        functools.partial(_peer_kernel, tt=tt, te=te, n_etiles=n_etiles),
        grid=(n_tiles + 2,),
        in_specs=[
            pl.BlockSpec((tt // 2, d), lambda k: (tile(k, 0)[0], 0)),
            pl.BlockSpec((te // 2, d), lambda k: (tile(k, 0)[1], 0)),
            pl.BlockSpec((d // 2, te), lambda k: (0, tile(k, 2)[1])),
            tab_spec, row_spec, row_spec, tab_spec,
            pl.BlockSpec((tt, d), lambda k: (tile(k, 2)[0], 0)),
            pl.BlockSpec((1, d), lambda k: (0, 0)),
        ],
        out_specs=pl.BlockSpec((tt, d), lambda k: (tile(k, 2)[0], 0)),
        out_shape=jax.ShapeDtypeStruct((n, d), F32),
        scratch_shapes=[
            pltpu.VMEM((te, tt), F32),
            pltpu.VMEM((te, tt), F32),
            pltpu.VMEM((te, tt), BF16),
            pltpu.VMEM((te, tt), BF16),
            pltpu.VMEM((d, tt), F32),
        ],
        compiler_params=pltpu.CompilerParams(
            dimension_semantics=("arbitrary",), vmem_limit_bytes=V7X_VMEM_LIMIT),
        name="peer",
    )(hb, u, vt, r2, n1, g1, g2, x2, final_norm_w)


def kernel(x, mem, mix_norm_w, w_in, conv_w, conv_norm_w, hgrn_lb_logits, hgrn_norm_w, w_out, xattn_norm_w,
           mem_norm_w, wq_mem, wkv_mem, wo_mem, ffn_norm_w, peer_w_query, peer_sub_keys, peer_down, peer_up,
           final_norm_w):
    b, t, d = x.shape
    depth = w_in.shape[0]
    assert depth == 1, "single-layer problem"
    assert t % 512 == 0 and d % V7X_LANES == 0
    row = lambda v: v.reshape(1, -1)
    x1 = _mixer(x, row(mix_norm_w[0]), w_in[0].astype(BF16), conv_w[0], row(conv_norm_w[0]), hgrn_lb_logits,
                row(hgrn_norm_w[0]), w_out[0].astype(BF16), tm=512)
    kv = _kv(mem, row(mem_norm_w[0]), wkv_mem[0].astype(BF16))
    x2 = _xattn(x1, row(xattn_norm_w[0]), wq_mem[0].astype(BF16), kv, wo_mem[0].astype(BF16), tm=512)
    x2 = x2.reshape(b * t, d)
    keys = peer_sub_keys[0].reshape(2 * PEER_HEADS, PEER_N_KEYS, PEER_DK).astype(BF16)
    hb, r2, n1, g1, g2 = _route(x2, row(ffn_norm_w[0]), peer_w_query[0].astype(BF16), keys, tm=512)
    u, vt = _tables(peer_down[0], peer_up[0], te=512)
    out = _peer(hb, u, vt, r2, n1, g1, g2, x2, row(final_norm_w), tt=512, te=1024)
    return out.reshape(b, t, d)
```

```python
import functools

import jax
import jax.numpy as jnp
from jax import lax
from jax.experimental import pallas as pl
from jax.experimental.pallas import tpu as pltpu

EPS = 1e-6
CONV_WIDTH = 512
HGRN_HEADS = 4
HGRN_HEAD_DIM = 128
HGRN_WIDTH = HGRN_HEADS * HGRN_HEAD_DIM
CHUNK = 64
MEM_HEADS = 4
PEER_HEADS = 8
PEER_N_KEYS = 128
PEER_DK = 128
PEER_TOPK = 16
V7X_LANES = 128
BF16_SUBLANES = 16
V7X_VMEM_LIMIT = 56 * 1024 * 1024
V7X_VMEM_LIMIT_PEER = 60 * 1024 * 1024

BF16 = jnp.bfloat16
F32 = jnp.float32
NT_DIMS = (((1,), (1,)), ((), ()))
TN_DIMS = (((0,), (0,)), ((), ()))


def _rms(x, w):
    return x * lax.rsqrt(jnp.mean(x * x, axis=-1, keepdims=True) + EPS) * w


def _sigmoid(x):
    return 1.0 / (1.0 + jnp.exp(-x))


def _mixer_kernel(x_ref, nw_ref, win_ref, convw_ref, cnw_ref, lbl_ref, hnw_ref, wout_ref, o_ref,
                  proj_ref, ohg_ref, st_ref, halo_ref, *, tm):
    cw, hw, dh = CONV_WIDTH, HGRN_WIDTH, HGRN_HEAD_DIM

    @pl.when(pl.program_id(1) == 0)
    def _():
        st_ref[...] = jnp.zeros_like(st_ref)
        halo_ref[...] = jnp.zeros_like(halo_ref)

    x = x_ref[0]
    h = _rms(x, nw_ref[...]).astype(BF16)
    proj_ref[...] = jnp.dot(h, win_ref[...], preferred_element_type=F32)

    bg = proj_ref[:, 0:cw]
    u = proj_ref[:, cw:2 * cw] * proj_ref[:, 2 * cw:3 * cw]
    row = lax.broadcasted_iota(jnp.int32, (tm, cw), 0)
    prev = halo_ref[...]
    u1 = jnp.where(row == 0, prev[7:8, :], pltpu.roll(u, 1, 0))
    u2 = jnp.where(row == 0, prev[6:7, :], jnp.where(row == 1, prev[7:8, :], pltpu.roll(u, 2, 0)))
    halo_ref[...] = u[tm - 8:tm, :]
    cwt = convw_ref[...]
    yc = _rms(bg * (u2 * cwt[0:1, :] + u1 * cwt[1:2, :] + u * cwt[2:3, :]), cnw_ref[...])

    lbl = lbl_ref[...]
    lmax = jnp.max(lbl, axis=0, keepdims=True)
    lexp = jnp.exp(lbl - lmax)
    lb = lexp[0:1, :] / jnp.sum(lexp, axis=0, keepdims=True)
    o0 = 3 * cw
    rowc = lax.broadcasted_iota(jnp.int32, (CHUNK, hw), 0)
    tril = (lax.broadcasted_iota(jnp.int32, (CHUNK, CHUNK), 0)
            >= lax.broadcasted_iota(jnp.int32, (CHUNK, CHUNK), 1))
    for c in range(tm // CHUNK):
        r0 = c * CHUNK
        qc = proj_ref[r0:r0 + CHUNK, o0:o0 + hw]
        fc = proj_ref[r0:r0 + CHUNK, o0 + hw:o0 + 2 * hw]
        ic = proj_ref[r0:r0 + CHUNK, o0 + 2 * hw:o0 + 3 * hw]
        qa = qc * _sigmoid(qc)
        fg = lb + (1.0 - lb) * _sigmoid(fc)
        kk = 1.0 - fg
        g = jnp.log(fg)
        for s in (1, 2, 4, 8, 16, 32):
            g = g + jnp.where(rowc >= s, pltpu.roll(g, s, 0), 0.0)
        gl = g[CHUNK - 1:CHUNK, :]
        q_dec = (qa * jnp.exp(g)).astype(BF16)
        k_inv = (kk * jnp.exp(-g)).astype(BF16)
        k_end = (kk * jnp.exp(gl - g)).astype(BF16)
        dec = jnp.exp(gl)
        vb = ic.astype(BF16)
        for hd in range(HGRN_HEADS):
            cs = slice(hd * dh, (hd + 1) * dh)
            a = lax.dot_general(q_dec[:, cs], k_inv[:, cs], NT_DIMS, preferred_element_type=F32)
            a = jnp.where(tril, a, 0.0).astype(BF16)
            st = st_ref[hd]
            o = (jnp.dot(a, vb[:, cs], preferred_element_type=F32)
                 + lax.dot_general(q_dec[:, cs], st.astype(BF16), NT_DIMS, preferred_element_type=F32))
            ohg_ref[r0:r0 + CHUNK, cs] = o
            ut = lax.dot_general(vb[:, cs], k_end[:, cs], TN_DIMS, preferred_element_type=F32)
            st_ref[hd] = st * dec[:, cs] + ut

    gate = proj_ref[:, o0 + 3 * hw:o0 + 4 * hw]
    gate = gate * _sigmoid(gate)
    hnw = hnw_ref[...]
    acc = x + jnp.dot(yc.astype(BF16), wout_ref[0:cw, :], preferred_element_type=F32)
    for hd in range(HGRN_HEADS):
        cs = slice(hd * dh, (hd + 1) * dh)
        oh = ohg_ref[:, cs]
        oh = oh * lax.rsqrt(jnp.mean(oh * oh, axis=-1, keepdims=True) + EPS)
        yh = oh * hnw[:, cs] * gate[:, cs]
        acc = acc + jnp.dot(yh.astype(BF16), wout_ref[cw + hd * dh:cw + (hd + 1) * dh, :],
                            preferred_element_type=F32)
    o_ref[0] = acc


def _mixer(x, mix_norm_w, w_in, conv_w, conv_norm_w, lb_logits, hgrn_norm_w, w_out, *, tm):
    b, t, d = x.shape
    ncols = w_in.shape[1]
    const = lambda *_: (0, 0)
    return pl.pallas_call(
        functools.partial(_mixer_kernel, tm=tm),
        grid=(b, t // tm),
        in_specs=[
            pl.BlockSpec((1, tm, d), lambda i, j: (i, j, 0)),
            pl.BlockSpec((1, d), const),
            pl.BlockSpec((d, ncols), const),
            pl.BlockSpec(conv_w.shape, const),
            pl.BlockSpec((1, CONV_WIDTH), const),
            pl.BlockSpec(lb_logits.shape, const),
            pl.BlockSpec((1, HGRN_WIDTH), const),
            pl.BlockSpec((d, d), const),
        ],
        out_specs=pl.BlockSpec((1, tm, d), lambda i, j: (i, j, 0)),
        out_shape=jax.ShapeDtypeStruct((b, t, d), F32),
        scratch_shapes=[
            pltpu.VMEM((tm, ncols), F32),
            pltpu.VMEM((tm, HGRN_WIDTH), F32),
            pltpu.VMEM((HGRN_HEADS, HGRN_HEAD_DIM, HGRN_HEAD_DIM), F32),
            pltpu.VMEM((8, CONV_WIDTH), F32),
        ],
        compiler_params=pltpu.CompilerParams(
            dimension_semantics=("arbitrary", "arbitrary"), vmem_limit_bytes=V7X_VMEM_LIMIT),
        name="mixer",
    )(x, mix_norm_w, w_in, conv_w, conv_norm_w, lb_logits, hgrn_norm_w, w_out)


def _kv_kernel(mem_ref, nw_ref, wkv_ref, o_ref):
    mn = _rms(mem_ref[0], nw_ref[...]).astype(BF16)
    o_ref[0] = jnp.dot(mn, wkv_ref[...], preferred_element_type=F32).astype(BF16)


def _kv(mem, mem_norm_w, wkv):
    b, m, d = mem.shape
    return pl.pallas_call(
        _kv_kernel,
        grid=(b,),
        in_specs=[
            pl.BlockSpec((1, m, d), lambda i: (i, 0, 0)),
            pl.BlockSpec((1, d), lambda i: (0, 0)),
            pl.BlockSpec((d, 2 * d), lambda i: (0, 0)),
        ],
        out_specs=pl.BlockSpec((1, m, 2 * d), lambda i: (i, 0, 0)),
        out_shape=jax.ShapeDtypeStruct((b, m, 2 * d), BF16),
        compiler_params=pltpu.CompilerParams(
            dimension_semantics=("arbitrary",), vmem_limit_bytes=V7X_VMEM_LIMIT),
        name="kv",
    )(mem, mem_norm_w, wkv)


def _xattn_kernel(x_ref, nw_ref, wq_ref, kv_ref, wo_ref, o_ref):
    x = x_ref[0]
    d = x.shape[-1]
    hd_dim = d // MEM_HEADS
    h = _rms(x, nw_ref[...]).astype(BF16)
    q = jnp.dot(h, wq_ref[...], preferred_element_type=F32)
    acc = x
    for hd in range(MEM_HEADS):
        cs = slice(hd * hd_dim, (hd + 1) * hd_dim)
        kh = kv_ref[0, :, cs]
        vh = kv_ref[0, :, d + hd * hd_dim:d + (hd + 1) * hd_dim]
        s = lax.dot_general(q[:, cs].astype(BF16), kh, NT_DIMS, preferred_element_type=F32)
        s = s * (hd_dim ** -0.5)
        p = jnp.exp(s - jnp.max(s, axis=-1, keepdims=True))
        p = p / jnp.sum(p, axis=-1, keepdims=True)
        oh = jnp.dot(p.astype(BF16), vh, preferred_element_type=F32)
        acc = acc + jnp.dot(oh.astype(BF16), wo_ref[cs, :], preferred_element_type=F32)
    o_ref[0] = acc


def _xattn(x, xattn_norm_w, wq, kv, wo, *, tm):
    b, t, d = x.shape
    m = kv.shape[1]
    const = lambda *_: (0, 0)
    return pl.pallas_call(
        _xattn_kernel,
        grid=(b, t // tm),
        in_specs=[
            pl.BlockSpec((1, tm, d), lambda i, j: (i, j, 0)),
            pl.BlockSpec((1, d), const),
            pl.BlockSpec((d, d), const),
            pl.BlockSpec((1, m, 2 * d), lambda i, j: (i, 0, 0)),
            pl.BlockSpec((d, d), const),
        ],
        out_specs=pl.BlockSpec((1, tm, d), lambda i, j: (i, j, 0)),
        out_shape=jax.ShapeDtypeStruct((b, t, d), F32),
        compiler_params=pltpu.CompilerParams(
            dimension_semantics=("arbitrary", "arbitrary"), vmem_limit_bytes=V7X_VMEM_LIMIT),
        name="xattn",
    )(x, xattn_norm_w, wq, kv, wo)


def _staircase():
    return [(q, PEER_TOPK // (q + 1)) for q in range(PEER_TOPK)]


N_CAND = sum(n for _, n in _staircase())
N_CAND_PAD = -(-N_CAND // 8) * 8


def _sorting_network(n):
    pairs = []

    def merge(lo, hi, r):
        step = r * 2
        if step < hi - lo:
            merge(lo, hi, step)
            merge(lo + r, hi, step)
            pairs.extend((i, i + r) for i in range(lo + r, hi - r, step))
        else:
            pairs.append((lo, lo + r))

    def sort(lo, hi):
        if hi > lo:
            mid = lo + (hi - lo) // 2
            sort(lo, mid)
            sort(mid + 1, hi)
            merge(lo, hi, 1)

    sort(0, n - 1)
    return pairs


def _top_values_sorted(rows, count, store):
    rows = list(rows)
    while len(rows) & (len(rows) - 1):
        rows.append(jnp.full_like(rows[0], -jnp.inf))
    for i, j in _sorting_network(len(rows)):
        rows[i], rows[j] = jnp.maximum(rows[i], rows[j]), jnp.minimum(rows[i], rows[j])
    m = None
    for r in range(count):
        m = jnp.max(rows[0], axis=0, keepdims=True)
        store(r, m)
        eq = rows[0] == m
        keep = min(len(rows), count - r - 1)
        for q in range(keep):
            nxt = rows[q + 1] if q + 1 < len(rows) else jnp.full_like(rows[q], -jnp.inf)
            rows[q] = jnp.where(eq, nxt, rows[q])
        rows = rows[:keep]
    return m


def _row_blocks(x):
    return [x[i:i + 8, :] for i in range(0, x.shape[0], 8)]


def _select(s1, s2, a_ref, b_ref, cand_ref, ls):
    def store_a(r, m):
        a_ref[r:r + 1, ls] = m

    def store_b(r, m):
        b_ref[r:r + 1, ls] = m

    _top_values_sorted(_row_blocks(s1), PEER_TOPK, store_a)
    _top_values_sorted(_row_blocks(s2), PEER_TOPK, store_b)
    rank2 = jnp.zeros(s2.shape, F32)
    for r in range(PEER_TOPK):
        rank2 = jnp.where(s2 < b_ref[r:r + 1, ls], float(r + 1), rank2)
    a = a_ref[:, ls]
    off = 0
    for qq, n in _staircase():
        cand_ref[off:off + n, ls] = a[0:n, :] + b_ref[qq:qq + 1, ls]
        off += n
    if N_CAND_PAD > N_CAND:
        cand_ref[N_CAND:N_CAND_PAD, ls] = jnp.full((N_CAND_PAD - N_CAND, a.shape[1]), -jnp.inf, F32)
    cand = cand_ref[:, ls]
    tau = _top_values_sorted(_row_blocks(cand), PEER_TOPK, lambda r, m: None)
    z = jnp.sum(jnp.where(cand >= tau, jnp.exp(cand - cand[0:1, :]), 0.0), axis=0, keepdims=True)
    n1 = jnp.zeros(s1.shape, F32)
    for qq in range(PEER_TOPK):
        ok = a + b_ref[qq:qq + 1, ls] >= tau
        theta = jnp.min(jnp.where(ok, a, jnp.inf), axis=0, keepdims=True)
        n1 = jnp.where(s1 >= theta, float(qq + 1), n1)
    g1 = jnp.exp(s1 - a[0:1, :])
    g2 = jnp.exp(s2 - b_ref[0:1, ls]) * (1.0 / z)
    return rank2, n1, g1, g2


def _bf16_bits(x):
    return pltpu.bitcast(x.astype(BF16).astype(F32), jnp.uint32)


def _dup_bf16_words(x):
    bits = _bf16_bits(x)
    return bits | lax.shift_right_logical(bits, jnp.uint32(16))


def _store_row_pairs(dst, x, slab_ref):
    rows, cols = x.shape
    for c in range(cols // V7X_LANES):
        cs = slice(c * V7X_LANES, (c + 1) * V7X_LANES)
        slab_ref[c] = x[:, cs]
        even = slab_ref[c, pl.ds(0, rows // 2, stride=2), :]
        odd = slab_ref[c, pl.ds(1, rows // 2, stride=2), :]
        dst[:, cs] = lax.shift_right_logical(_bf16_bits(even), jnp.uint32(16)) | _bf16_bits(odd)


def _route_kernel(x_ref, nw_ref, wqry_ref, keys_ref, hb_ref, r2_ref, n1_ref, g1_ref, g2_ref,
                  a_ref, b_ref, cand_ref, hslab_ref, kslab_ref, s1_ref, s2_ref):
    hf = _rms(x_ref[...], nw_ref[...])
    tm = hf.shape[0]
    _store_row_pairs(hb_ref, hf, hslab_ref)
    q = jnp.dot(hf.astype(BF16), wqry_ref[...], preferred_element_type=F32).astype(BF16)
    for hd in range(PEER_HEADS):
        c1 = (2 * hd) * PEER_DK
        c2 = (2 * hd + 1) * PEER_DK
        s1_ref[...] = lax.dot_general(keys_ref[2 * hd], q[:, c1:c1 + PEER_DK], NT_DIMS,
                                      preferred_element_type=F32)
        s2_ref[...] = lax.dot_general(keys_ref[2 * hd + 1], q[:, c2:c2 + PEER_DK], NT_DIMS,
                                      preferred_element_type=F32)
        for lt in range(tm // V7X_LANES):
            ls = slice(lt * V7X_LANES, (lt + 1) * V7X_LANES)
            rank2, n1, g1, g2 = _select(s1_ref[:, ls], s2_ref[:, ls], a_ref, b_ref, cand_ref, ls)
            _store_row_pairs(r2_ref.at[hd, :, ls], rank2, kslab_ref.at[pl.ds(2 * lt, 1)])
            n1_ref[hd, :, ls] = _dup_bf16_words(n1)
            g1_ref[hd, :, ls] = _dup_bf16_words(g1)
            _store_row_pairs(g2_ref.at[hd, :, ls], g2, kslab_ref.at[pl.ds(2 * lt + 1, 1)])


def _route(x2, ffn_norm_w, w_query, keys, *, tm):
    n, d = x2.shape
    nq = w_query.shape[1]
    tab = jax.ShapeDtypeStruct((PEER_HEADS, PEER_N_KEYS, n), jnp.uint32)
    tabh = jax.ShapeDtypeStruct((PEER_HEADS, PEER_N_KEYS // 2, n), jnp.uint32)
    tab_spec = pl.BlockSpec((PEER_HEADS, PEER_N_KEYS, tm), lambda i: (0, 0, i))
    tabh_spec = pl.BlockSpec((PEER_HEADS, PEER_N_KEYS // 2, tm), lambda i: (0, 0, i))
    return pl.pallas_call(
        _route_kernel,
        grid=(n // tm,),
        in_specs=[
            pl.BlockSpec((tm, d), lambda i: (i, 0)),
            pl.BlockSpec((1, d), lambda i: (0, 0)),
            pl.BlockSpec((d, nq), lambda i: (0, 0)),
            pl.BlockSpec(keys.shape, lambda i: (0, 0, 0)),
        ],
        out_specs=[pl.BlockSpec((tm // 2, d), lambda i: (i, 0)), tabh_spec, tab_spec, tab_spec, tabh_spec],
        out_shape=[jax.ShapeDtypeStruct((n // 2, d), jnp.uint32), tabh, tab, tab, tabh],
        scratch_shapes=[
            pltpu.VMEM((PEER_TOPK, tm), F32),
            pltpu.VMEM((PEER_TOPK, tm), F32),
            pltpu.VMEM((N_CAND_PAD, tm), F32),
            pltpu.VMEM((d // V7X_LANES, tm, V7X_LANES), F32),
            pltpu.VMEM((2 * tm // V7X_LANES, PEER_N_KEYS, V7X_LANES), F32),
            pltpu.VMEM((PEER_N_KEYS, tm), F32),
            pltpu.VMEM((PEER_N_KEYS, tm), F32),
        ],
        compiler_params=pltpu.CompilerParams(
            dimension_semantics=("arbitrary",), vmem_limit_bytes=V7X_VMEM_LIMIT),
        name="route",
    )(x2, ffn_norm_w, w_query, keys)


def _tables_kernel(down_ref, up_ref, u_ref, vt_ref, uslab_ref, vslab_ref):
    _store_row_pairs(u_ref, down_ref[...], uslab_ref)
    _store_row_pairs(vt_ref, up_ref[...].T, vslab_ref)


def _tables(peer_down, peer_up, *, te):
    ne, d = peer_down.shape
    return pl.pallas_call(
        _tables_kernel,
        grid=(ne // te,),
        in_specs=[pl.BlockSpec((te, d), lambda i: (i, 0)), pl.BlockSpec((te, d), lambda i: (i, 0))],
        out_specs=[pl.BlockSpec((te // 2, d), lambda i: (i, 0)), pl.BlockSpec((d // 2, te), lambda i: (0, i))],
        out_shape=[jax.ShapeDtypeStruct((ne // 2, d), jnp.uint32), jax.ShapeDtypeStruct((d // 2, ne), jnp.uint32)],
        scratch_shapes=[pltpu.VMEM((d // V7X_LANES, te, V7X_LANES), F32),
                        pltpu.VMEM((te // V7X_LANES, d, V7X_LANES), F32)],
        compiler_params=pltpu.CompilerParams(
            dimension_semantics=("arbitrary",), vmem_limit_bytes=V7X_VMEM_LIMIT),
        name="tables",
    )(peer_down, peer_up)


def _divmod_nonneg(t, n):
    if n & (n - 1) == 0:
        return lax.shift_right_logical(t, n.bit_length() - 1), t & (n - 1)
    return t // n, lax.rem(t, n)


def _peer_kernel(hb_ref, u_ref, vt_ref, r2_ref, n1_ref, g1_ref, g2_ref, x2_ref, fnw_ref, o_ref,
                 s0_ref, s1_ref, h0_ref, h1_ref, acc_ref, *, tt, te, n_etiles):
    k = pl.program_id(0)
    kc = k - 2
    ec = _divmod_nonneg(jnp.maximum(kc, 0), n_etiles)[1]
    nk = PEER_N_KEYS
    lw = 2 * V7X_LANES

    @pl.when(k == 0)
    def _():
        for ref in (s0_ref, s1_ref, h0_ref, h1_ref):
            ref[...] = jnp.zeros_like(ref)

    @pl.when((k == 0) | ((kc >= 0) & (ec == 0)))
    def _():
        acc_ref[...] = jnp.zeros_like(acc_ref)

    mc = 2 * nk
    d = acc_ref.shape[0]
    pieces_a = [(m, l) for l in range(tt // lw) for m in range(te // mc)]
    pieces_c = [(m, l) for l in range(tt // lw) for m in range(d // mc)]
    nb_rows = nk // 2
    ii_group = 2
    blocks_b = [(ig, jh, l) for l in range(tt // lw) for ig in range(te // nk // ii_group)
                for jh in range(nk // nb_rows)]

    def words(ref, *idx):
        return pltpu.bitcast(ref[idx], BF16)

    def stages(sa_ref, sb_ref, hb_out_ref, hc_ref):
        def stage_a(m, l):
            rs, ls = slice(m * mc, (m + 1) * mc), slice(l * lw, (l + 1) * lw)
            u = words(u_ref, slice(m * mc // 2, (m + 1) * mc // 2), slice(None))
            hb = words(hb_ref, slice(l * lw // 2, (l + 1) * lw // 2), slice(None))
            sa_ref[rs, ls] = lax.dot_general(u, hb, NT_DIMS, preferred_element_type=F32)

        def stage_b(ig, jh, l):
            ls = slice(l * lw, (l + 1) * lw)
            js = slice(jh * nb_rows // 2, (jh + 1) * nb_rows // 2)
            iis = range(ig * ii_group, (ig + 1) * ii_group)
            w = [jnp.zeros((nb_rows, lw), BF16) for _ in iis]
            for hd in range(PEER_HEADS):
                g2 = words(g2_ref, hd, js, ls)
                r2 = words(r2_ref, hd, js, ls)
                for t, ii in enumerate(iis):
                    n1 = pltpu.bitcast(jnp.broadcast_to(n1_ref[hd, ii:ii + 1, ls], (nb_rows // 2, lw)), BF16)
                    g1 = pltpu.bitcast(jnp.broadcast_to(g1_ref[hd, ii:ii + 1, ls], (nb_rows // 2, lw)), BF16)
                    w[t] = w[t] + jnp.where(r2 < n1, g2 * g1, jnp.zeros_like(g2))
            for t, ii in enumerate(iis):
                rs = slice(ii * nk + jh * nb_rows, ii * nk + (jh + 1) * nb_rows)
                s = sb_ref[rs, ls]
                act = 0.5 * s * (1.0 + lax.erf(s * 0.7071067811865476))
                hb_out_ref[rs, ls] = w[t] * act.astype(BF16)

        def stage_c(m, l):
            rs, ls = slice(m * mc, (m + 1) * mc), slice(l * lw, (l + 1) * lw)
            vt = words(vt_ref, slice(m * mc // 2, (m + 1) * mc // 2), slice(None))
            acc_ref[rs, ls] += jnp.dot(vt, hc_ref[:, ls], preferred_element_type=F32)

        nb = len(blocks_b) // len(pieces_a)
        for p in range(len(pieces_a)):
            for q in range(nb):
                stage_b(*blocks_b[p * nb + q])
            stage_a(*pieces_a[p])
            stage_c(*pieces_c[p])

    @pl.when(lax.rem(k, 2) == 0)
    def _():
        stages(s0_ref, s1_ref, h1_ref, h0_ref)

    @pl.when(lax.rem(k, 2) == 1)
    def _():
        stages(s1_ref, s0_ref, h0_ref, h1_ref)

    @pl.when((kc >= 0) & (ec == n_etiles - 1))
    def _():
        z = x2_ref[...] + acc_ref[...].T
        o_ref[...] = _rms(z, fnw_ref[...])


def _peer(hb, u, vt, r2, n1, g1, g2, x2, final_norm_w, *, tt, te):
    n, d = x2.shape
    ne = vt.shape[1]
    n_ttiles, n_etiles = n // tt, ne // te
    n_tiles = n_ttiles * n_etiles
    ni = te // PEER_N_KEYS
    assert ni == 8, "the first-half key rows of one expert tile must be one 32-bit sublane group"

    def tile(k, lag):
        return _divmod_nonneg(jnp.clip(k - lag, 0, n_tiles - 1), n_etiles)

    tab_spec = pl.BlockSpec((PEER_HEADS, PEER_N_KEYS // 2, tt), lambda k: (0, 0, tile(k, 1)[0]))
    row_spec = pl.BlockSpec((PEER_HEADS, ni, tt), lambda k: (0, tile(k, 1)[1], tile(k, 1)[0]))
    return pl.pallas_call(
        functools.partial(_peer_kernel, tt=tt, te=te, n_etiles=n_etiles),
        grid=(n_tiles + 2,),
        in_specs=[
            pl.BlockSpec((tt // 2, d), lambda k: (tile(k, 0)[0], 0)),
            pl.BlockSpec((te // 2, d), lambda k: (tile(k, 0)[1], 0)),
            pl.BlockSpec((d // 2, te), lambda k: (0, tile(k, 2)[1])),
            tab_spec, row_spec, row_spec, tab_spec,
            pl.BlockSpec((tt, d), lambda k: (tile(k, 2)[0], 0)),
            pl.BlockSpec((1, d), lambda k: (0, 0)),
        ],
        out_specs=pl.BlockSpec((tt, d), lambda k: (tile(k, 2)[0], 0)),
        out_shape=jax.ShapeDtypeStruct((n, d), F32),
        scratch_shapes=[
            pltpu.VMEM((te, tt), F32),
            pltpu.VMEM((te, tt), F32),
            pltpu.VMEM((te, tt), BF16),
            pltpu.VMEM((te, tt), BF16),
            pltpu.VMEM((d, tt), F32),
        ],
        compiler_params=pltpu.CompilerParams(
            dimension_semantics=("arbitrary",), vmem_limit_bytes=V7X_VMEM_LIMIT_PEER),
        name="peer",
    )(hb, u, vt, r2, n1, g1, g2, x2, final_norm_w)


def kernel(x, mem, mix_norm_w, w_in, conv_w, conv_norm_w, hgrn_lb_logits, hgrn_norm_w, w_out, xattn_norm_w,
           mem_norm_w, wq_mem, wkv_mem, wo_mem, ffn_norm_w, peer_w_query, peer_sub_keys, peer_down, peer_up,
           final_norm_w):
    b, t, d = x.shape
    depth = w_in.shape[0]
    assert depth == 1, "single-layer problem"
    assert t % 512 == 0 and d % V7X_LANES == 0
    row = lambda v: v.reshape(1, -1)
    x1 = _mixer(x, row(mix_norm_w[0]), w_in[0].astype(BF16), conv_w[0], row(conv_norm_w[0]), hgrn_lb_logits,
                row(hgrn_norm_w[0]), w_out[0].astype(BF16), tm=512)
    kv = _kv(mem, row(mem_norm_w[0]), wkv_mem[0].astype(BF16))
    x2 = _xattn(x1, row(xattn_norm_w[0]), wq_mem[0].astype(BF16), kv, wo_mem[0].astype(BF16), tm=512)
    x2 = x2.reshape(b * t, d)
    keys = peer_sub_keys[0].reshape(2 * PEER_HEADS, PEER_N_KEYS, PEER_DK).astype(BF16)
    hb, r2, n1, g1, g2 = _route(x2, row(ffn_norm_w[0]), peer_w_query[0].astype(BF16), keys, tm=512)
    u, vt = _tables(peer_down[0], peer_up[0], te=512)
    out = _peer(hb, u, vt, r2, n1, g1, g2, x2, row(final_norm_w), tt=1024, te=1024)
    return out.reshape(b, t, d)
```

```python
import functools

import jax
import jax.numpy as jnp
from jax import lax
from jax.experimental import pallas as pl
from jax.experimental.pallas import tpu as pltpu

EPS = 1e-6
CONV_WIDTH = 512
HGRN_HEADS = 4
HGRN_HEAD_DIM = 128
HGRN_WIDTH = HGRN_HEADS * HGRN_HEAD_DIM
CHUNK = 64
MEM_HEADS = 4
PEER_HEADS = 8
PEER_N_KEYS = 128
PEER_DK = 128
PEER_TOPK = 16
V7X_LANES = 128
BF16_SUBLANES = 16
V7X_VMEM_LIMIT = 56 * 1024 * 1024

BF16 = jnp.bfloat16
F32 = jnp.float32
NT_DIMS = (((1,), (1,)), ((), ()))
TN_DIMS = (((0,), (0,)), ((), ()))


def _rms(x, w):
    return x * lax.rsqrt(jnp.mean(x * x, axis=-1, keepdims=True) + EPS) * w


def _sigmoid(x):
    return 1.0 / (1.0 + jnp.exp(-x))


def _mixer_kernel(x_ref, nw_ref, win_ref, convw_ref, cnw_ref, lbl_ref, hnw_ref, wout_ref, o_ref,
                  proj_ref, ohg_ref, st_ref, halo_ref, *, tm):
    cw, hw, dh = CONV_WIDTH, HGRN_WIDTH, HGRN_HEAD_DIM

    @pl.when(pl.program_id(1) == 0)
    def _():
        st_ref[...] = jnp.zeros_like(st_ref)
        halo_ref[...] = jnp.zeros_like(halo_ref)

    x = x_ref[0]
    h = _rms(x, nw_ref[...]).astype(BF16)
    proj_ref[...] = jnp.dot(h, win_ref[...], preferred_element_type=F32)

    bg = proj_ref[:, 0:cw]
    u = proj_ref[:, cw:2 * cw] * proj_ref[:, 2 * cw:3 * cw]
    row = lax.broadcasted_iota(jnp.int32, (tm, cw), 0)
    prev = halo_ref[...]
    u1 = jnp.where(row == 0, prev[7:8, :], pltpu.roll(u, 1, 0))
    u2 = jnp.where(row == 0, prev[6:7, :], jnp.where(row == 1, prev[7:8, :], pltpu.roll(u, 2, 0)))
    halo_ref[...] = u[tm - 8:tm, :]
    cwt = convw_ref[...]
    yc = _rms(bg * (u2 * cwt[0:1, :] + u1 * cwt[1:2, :] + u * cwt[2:3, :]), cnw_ref[...])

    lbl = lbl_ref[...]
    lmax = jnp.max(lbl, axis=0, keepdims=True)
    lexp = jnp.exp(lbl - lmax)
    lb = lexp[0:1, :] / jnp.sum(lexp, axis=0, keepdims=True)
    o0 = 3 * cw
    rowc = lax.broadcasted_iota(jnp.int32, (CHUNK, hw), 0)
    tril = (lax.broadcasted_iota(jnp.int32, (CHUNK, CHUNK), 0)
            >= lax.broadcasted_iota(jnp.int32, (CHUNK, CHUNK), 1))
    hcols = [slice(hd * dh, (hd + 1) * dh) for hd in range(HGRN_HEADS)]

    def chunk_local(c):
        r0 = c * CHUNK
        qc = proj_ref[r0:r0 + CHUNK, o0:o0 + hw]
        fc = proj_ref[r0:r0 + CHUNK, o0 + hw:o0 + 2 * hw]
        ic = proj_ref[r0:r0 + CHUNK, o0 + 2 * hw:o0 + 3 * hw]
        qa = qc * _sigmoid(qc)
        fg = lb + (1.0 - lb) * _sigmoid(fc)
        kk = 1.0 - fg
        g = jnp.log(fg)
        for s in (1, 2, 4, 8, 16, 32):
            g = g + jnp.where(rowc >= s, pltpu.roll(g, s, 0), 0.0)
        gl = g[CHUNK - 1:CHUNK, :]
        q_dec = (qa * jnp.exp(g)).astype(BF16)
        k_inv = (kk * jnp.exp(-g)).astype(BF16)
        k_end = (kk * jnp.exp(gl - g)).astype(BF16)
        vb = ic.astype(BF16)
        att = [lax.dot_general(q_dec[:, cs], k_inv[:, cs], NT_DIMS, preferred_element_type=F32) for cs in hcols]
        uts = [lax.dot_general(vb[:, cs], k_end[:, cs], TN_DIMS, preferred_element_type=F32) for cs in hcols]
        intra = [jnp.dot(jnp.where(tril, a, 0.0).astype(BF16), vb[:, cs], preferred_element_type=F32)
                 for a, cs in zip(att, hcols)]
        return q_dec, jnp.exp(gl), uts, intra

    nxt = chunk_local(0)
    for c in range(tm // CHUNK):
        r0 = c * CHUNK
        q_dec, dec, uts, intra = nxt
        if c + 1 < tm // CHUNK:
            nxt = chunk_local(c + 1)
        for hd, cs in enumerate(hcols):
            st = st_ref[hd]
            ohg_ref[r0:r0 + CHUNK, cs] = intra[hd] + lax.dot_general(
                q_dec[:, cs], st.astype(BF16), NT_DIMS, preferred_element_type=F32)
            st_ref[hd] = st * dec[:, cs] + uts[hd]

    gate = proj_ref[:, o0 + 3 * hw:o0 + 4 * hw]
    gate = gate * _sigmoid(gate)
    hnw = hnw_ref[...]
    ys = [yc.astype(BF16)]
    for hd in range(HGRN_HEADS):
        cs = slice(hd * dh, (hd + 1) * dh)
        oh = ohg_ref[:, cs]
        oh = oh * lax.rsqrt(jnp.mean(oh * oh, axis=-1, keepdims=True) + EPS)
        ys.append((oh * hnw[:, cs] * gate[:, cs]).astype(BF16))
    o_ref[0] = x + jnp.dot(jnp.concatenate(ys, axis=1), wout_ref[...], preferred_element_type=F32)


def _mixer(x, mix_norm_w, w_in, conv_w, conv_norm_w, lb_logits, hgrn_norm_w, w_out, *, tm):
    b, t, d = x.shape
    ncols = w_in.shape[1]
    const = lambda *_: (0, 0)
    return pl.pallas_call(
        functools.partial(_mixer_kernel, tm=tm),
        grid=(b, t // tm),
        in_specs=[
            pl.BlockSpec((1, tm, d), lambda i, j: (i, j, 0)),
            pl.BlockSpec((1, d), const),
            pl.BlockSpec((d, ncols), const),
            pl.BlockSpec(conv_w.shape, const),
            pl.BlockSpec((1, CONV_WIDTH), const),
            pl.BlockSpec(lb_logits.shape, const),
            pl.BlockSpec((1, HGRN_WIDTH), const),
            pl.BlockSpec((d, d), const),
        ],
        out_specs=pl.BlockSpec((1, tm, d), lambda i, j: (i, j, 0)),
        out_shape=jax.ShapeDtypeStruct((b, t, d), F32),
        scratch_shapes=[
            pltpu.VMEM((tm, ncols), F32),
            pltpu.VMEM((tm, HGRN_WIDTH), F32),
            pltpu.VMEM((HGRN_HEADS, HGRN_HEAD_DIM, HGRN_HEAD_DIM), F32),
            pltpu.VMEM((8, CONV_WIDTH), F32),
        ],
        compiler_params=pltpu.CompilerParams(
            dimension_semantics=("arbitrary", "arbitrary"), vmem_limit_bytes=V7X_VMEM_LIMIT),
        name="mixer",
    )(x, mix_norm_w, w_in, conv_w, conv_norm_w, lb_logits, hgrn_norm_w, w_out)


def _kv_kernel(mem_ref, nw_ref, wkv_ref, o_ref):
    mn = _rms(mem_ref[0], nw_ref[...]).astype(BF16)
    o_ref[0] = jnp.dot(mn, wkv_ref[...], preferred_element_type=F32).astype(BF16)


def _kv(mem, mem_norm_w, wkv):
    b, m, d = mem.shape
    return pl.pallas_call(
        _kv_kernel,
        grid=(b,),
        in_specs=[
            pl.BlockSpec((1, m, d), lambda i: (i, 0, 0)),
            pl.BlockSpec((1, d), lambda i: (0, 0)),
            pl.BlockSpec((d, 2 * d), lambda i: (0, 0)),
        ],
        out_specs=pl.BlockSpec((1, m, 2 * d), lambda i: (i, 0, 0)),
        out_shape=jax.ShapeDtypeStruct((b, m, 2 * d), BF16),
        compiler_params=pltpu.CompilerParams(
            dimension_semantics=("arbitrary",), vmem_limit_bytes=V7X_VMEM_LIMIT),
        name="kv",
    )(mem, mem_norm_w, wkv)


def _xattn_kernel(x_ref, nw_ref, wq_ref, kv_ref, wo_ref, o_ref):
    x = x_ref[0]
    d = x.shape[-1]
    hd_dim = d // MEM_HEADS
    h = _rms(x, nw_ref[...]).astype(BF16)
    q = jnp.dot(h, wq_ref[...], preferred_element_type=F32)
    heads = [slice(hd * hd_dim, (hd + 1) * hd_dim) for hd in range(MEM_HEADS)]
    scores = [lax.dot_general(q[:, cs].astype(BF16), kv_ref[0, :, cs], NT_DIMS, preferred_element_type=F32)
              for cs in heads]
    outs = []
    for hd, cs in enumerate(heads):
        vh = kv_ref[0, :, d + hd * hd_dim:d + (hd + 1) * hd_dim]
        s = scores[hd] * (hd_dim ** -0.5)
        p = jnp.exp(s - jnp.max(s, axis=-1, keepdims=True))
        p = p / jnp.sum(p, axis=-1, keepdims=True)
        outs.append(jnp.dot(p.astype(BF16), vh, preferred_element_type=F32).astype(BF16))
    o_ref[0] = x + jnp.dot(jnp.concatenate(outs, axis=1), wo_ref[...], preferred_element_type=F32)


def _xattn(x, xattn_norm_w, wq, kv, wo, *, tm):
    b, t, d = x.shape
    m = kv.shape[1]
    const = lambda *_: (0, 0)
    return pl.pallas_call(
        _xattn_kernel,
        grid=(b, t // tm),
        in_specs=[
            pl.BlockSpec((1, tm, d), lambda i, j: (i, j, 0)),
            pl.BlockSpec((1, d), const),
            pl.BlockSpec((d, d), const),
            pl.BlockSpec((1, m, 2 * d), lambda i, j: (i, 0, 0)),
            pl.BlockSpec((d, d), const),
        ],
        out_specs=pl.BlockSpec((1, tm, d), lambda i, j: (i, j, 0)),
        out_shape=jax.ShapeDtypeStruct((b, t, d), F32),
        compiler_params=pltpu.CompilerParams(
            dimension_semantics=("arbitrary", "arbitrary"), vmem_limit_bytes=V7X_VMEM_LIMIT),
        name="xattn",
    )(x, xattn_norm_w, wq, kv, wo)


def _staircase():
    return [(q, PEER_TOPK // (q + 1)) for q in range(PEER_TOPK)]


N_CAND = sum(n for _, n in _staircase())
N_CAND_PAD = -(-N_CAND // 8) * 8


def _sorting_network(n):
    pairs = []

    def merge(lo, hi, r):
        step = r * 2
        if step < hi - lo:
            merge(lo, hi, step)
            merge(lo + r, hi, step)
            pairs.extend((i, i + r) for i in range(lo + r, hi - r, step))
        else:
            pairs.append((lo, lo + r))

    def sort(lo, hi):
        if hi > lo:
            mid = lo + (hi - lo) // 2
            sort(lo, mid)
            sort(mid + 1, hi)
            merge(lo, hi, 1)

    sort(0, n - 1)
    return pairs


def _top_values_sorted(rows, count, store):
    rows = list(rows)
    while len(rows) & (len(rows) - 1):
        rows.append(jnp.full_like(rows[0], -jnp.inf))
    for i, j in _sorting_network(len(rows)):
        rows[i], rows[j] = jnp.maximum(rows[i], rows[j]), jnp.minimum(rows[i], rows[j])
    m = None
    for r in range(count):
        m = jnp.max(rows[0], axis=0, keepdims=True)
        store(r, m)
        eq = rows[0] == m
        keep = min(len(rows), count - r - 1)
        for q in range(keep):
            nxt = rows[q + 1] if q + 1 < len(rows) else jnp.full_like(rows[q], -jnp.inf)
            rows[q] = jnp.where(eq, nxt, rows[q])
        rows = rows[:keep]
    return m


def _row_blocks(x):
    return [x[i:i + 8, :] for i in range(0, x.shape[0], 8)]


def _select(s1, s2, a_ref, b_ref, cand_ref, ls):
    def store_a(r, m):
        a_ref[r:r + 1, ls] = m

    def store_b(r, m):
        b_ref[r:r + 1, ls] = m

    _top_values_sorted(_row_blocks(s1), PEER_TOPK, store_a)
    _top_values_sorted(_row_blocks(s2), PEER_TOPK, store_b)
    rank2 = jnp.zeros(s2.shape, F32)
    for r in range(PEER_TOPK):
        rank2 = jnp.where(s2 < b_ref[r:r + 1, ls], float(r + 1), rank2)
    a = a_ref[:, ls]
    off = 0
    for qq, n in _staircase():
        cand_ref[off:off + n, ls] = a[0:n, :] + b_ref[qq:qq + 1, ls]
        off += n
    if N_CAND_PAD > N_CAND:
        cand_ref[N_CAND:N_CAND_PAD, ls] = jnp.full((N_CAND_PAD - N_CAND, a.shape[1]), -jnp.inf, F32)
    cand = cand_ref[:, ls]
    tau = _top_values_sorted(_row_blocks(cand), PEER_TOPK, lambda r, m: None)
    z = jnp.sum(jnp.where(cand >= tau, jnp.exp(cand - cand[0:1, :]), 0.0), axis=0, keepdims=True)
    n1 = jnp.zeros(s1.shape, F32)
    for qq in range(PEER_TOPK):
        ok = a + b_ref[qq:qq + 1, ls] >= tau
        theta = jnp.min(jnp.where(ok, a, jnp.inf), axis=0, keepdims=True)
        n1 = jnp.where(s1 >= theta, float(qq + 1), n1)
    g1 = jnp.exp(s1 - a[0:1, :])
    g2 = jnp.exp(s2 - b_ref[0:1, ls]) * (1.0 / z)
    return rank2, n1, g1, g2


def _bf16_bits(x):
    return pltpu.bitcast(x.astype(BF16).astype(F32), jnp.uint32)


def _dup_bf16_words(x):
    bits = _bf16_bits(x)
    return bits | lax.shift_right_logical(bits, jnp.uint32(16))


def _store_row_pairs(dst, x, slab_ref):
    rows, cols = x.shape
    for c in range(cols // V7X_LANES):
        cs = slice(c * V7X_LANES, (c + 1) * V7X_LANES)
        slab_ref[c] = x[:, cs]
        even = slab_ref[c, pl.ds(0, rows // 2, stride=2), :]
        odd = slab_ref[c, pl.ds(1, rows // 2, stride=2), :]
        dst[:, cs] = lax.shift_right_logical(_bf16_bits(even), jnp.uint32(16)) | _bf16_bits(odd)


def _route_kernel(x_ref, nw_ref, wqry_ref, keys_ref, hb_ref, r2_ref, n1_ref, g1_ref, g2_ref,
                  a_ref, b_ref, cand_ref, hslab_ref, kslab_ref, s1_ref, s2_ref):
    hf = _rms(x_ref[...], nw_ref[...])
    tm = hf.shape[0]
    _store_row_pairs(hb_ref, hf, hslab_ref)
    q = jnp.dot(hf.astype(BF16), wqry_ref[...], preferred_element_type=F32).astype(BF16)
    for hd in range(PEER_HEADS):
        c1 = (2 * hd) * PEER_DK
        c2 = (2 * hd + 1) * PEER_DK
        s1_ref[...] = lax.dot_general(keys_ref[2 * hd], q[:, c1:c1 + PEER_DK], NT_DIMS,
                                      preferred_element_type=F32)
        s2_ref[...] = lax.dot_general(keys_ref[2 * hd + 1], q[:, c2:c2 + PEER_DK], NT_DIMS,
                                      preferred_element_type=F32)
        for lt in range(tm // V7X_LANES):
            ls = slice(lt * V7X_LANES, (lt + 1) * V7X_LANES)
            rank2, n1, g1, g2 = _select(s1_ref[:, ls], s2_ref[:, ls], a_ref, b_ref, cand_ref, ls)
            _store_row_pairs(r2_ref.at[hd, :, ls], rank2, kslab_ref.at[pl.ds(2 * lt, 1)])
            n1_ref[hd, :, ls] = _dup_bf16_words(n1)
            g1_ref[hd, :, ls] = _dup_bf16_words(g1)
            _store_row_pairs(g2_ref.at[hd, :, ls], g2, kslab_ref.at[pl.ds(2 * lt + 1, 1)])


def _route(x2, ffn_norm_w, w_query, keys, *, tm):
    n, d = x2.shape
    nq = w_query.shape[1]
    tab = jax.ShapeDtypeStruct((PEER_HEADS, PEER_N_KEYS, n), jnp.uint32)
    tabh = jax.ShapeDtypeStruct((PEER_HEADS, PEER_N_KEYS // 2, n), jnp.uint32)
    tab_spec = pl.BlockSpec((PEER_HEADS, PEER_N_KEYS, tm), lambda i: (0, 0, i))
    tabh_spec = pl.BlockSpec((PEER_HEADS, PEER_N_KEYS // 2, tm), lambda i: (0, 0, i))
    return pl.pallas_call(
        _route_kernel,
        grid=(n // tm,),
        in_specs=[
            pl.BlockSpec((tm, d), lambda i: (i, 0)),
            pl.BlockSpec((1, d), lambda i: (0, 0)),
            pl.BlockSpec((d, nq), lambda i: (0, 0)),
            pl.BlockSpec(keys.shape, lambda i: (0, 0, 0)),
        ],
        out_specs=[pl.BlockSpec((tm // 2, d), lambda i: (i, 0)), tabh_spec, tab_spec, tab_spec, tabh_spec],
        out_shape=[jax.ShapeDtypeStruct((n // 2, d), jnp.uint32), tabh, tab, tab, tabh],
        scratch_shapes=[
            pltpu.VMEM((PEER_TOPK, tm), F32),
            pltpu.VMEM((PEER_TOPK, tm), F32),
            pltpu.VMEM((N_CAND_PAD, tm), F32),
            pltpu.VMEM((d // V7X_LANES, tm, V7X_LANES), F32),
            pltpu.VMEM((2 * tm // V7X_LANES, PEER_N_KEYS, V7X_LANES), F32),
            pltpu.VMEM((PEER_N_KEYS, tm), F32),
            pltpu.VMEM((PEER_N_KEYS, tm), F32),
        ],
        compiler_params=pltpu.CompilerParams(
            dimension_semantics=("arbitrary",), vmem_limit_bytes=V7X_VMEM_LIMIT),
        name="route",
    )(x2, ffn_norm_w, w_query, keys)


def _tables_kernel(down_ref, up_ref, u_ref, vt_ref, uslab_ref, vslab_ref):
    _store_row_pairs(u_ref, down_ref[...], uslab_ref)
    _store_row_pairs(vt_ref, up_ref[...].T, vslab_ref)


def _tables(peer_down, peer_up, *, te):
    ne, d = peer_down.shape
    return pl.pallas_call(
        _tables_kernel,
        grid=(ne // te,),
        in_specs=[pl.BlockSpec((te, d), lambda i: (i, 0)), pl.BlockSpec((te, d), lambda i: (i, 0))],
        out_specs=[pl.BlockSpec((te // 2, d), lambda i: (i, 0)), pl.BlockSpec((d // 2, te), lambda i: (0, i))],
        out_shape=[jax.ShapeDtypeStruct((ne // 2, d), jnp.uint32), jax.ShapeDtypeStruct((d // 2, ne), jnp.uint32)],
        scratch_shapes=[pltpu.VMEM((d // V7X_LANES, te, V7X_LANES), F32),
                        pltpu.VMEM((te // V7X_LANES, d, V7X_LANES), F32)],
        compiler_params=pltpu.CompilerParams(
            dimension_semantics=("arbitrary",), vmem_limit_bytes=V7X_VMEM_LIMIT),
        name="tables",
    )(peer_down, peer_up)


def _divmod_nonneg(t, n):
    if n & (n - 1) == 0:
        return lax.shift_right_logical(t, n.bit_length() - 1), t & (n - 1)
    return t // n, lax.rem(t, n)


def _peer_kernel(hb_ref, u_ref, vt_ref, r2_ref, n1_ref, g1_ref, g2_ref, x2_ref, fnw_ref, o_ref,
                 s0_ref, s1_ref, h0_ref, h1_ref, acc_ref, *, tt, te, n_etiles):
    k = pl.program_id(0)
    kc = k - 2
    ec = _divmod_nonneg(jnp.maximum(kc, 0), n_etiles)[1]
    nk = PEER_N_KEYS
    lw = 2 * V7X_LANES

    @pl.when(k == 0)
    def _():
        for ref in (s0_ref, s1_ref, h0_ref, h1_ref):
            ref[...] = jnp.zeros_like(ref)

    @pl.when((k == 0) | ((kc >= 0) & (ec == 0)))
    def _():
        acc_ref[...] = jnp.zeros_like(acc_ref)

    mc = 2 * nk
    d = acc_ref.shape[0]
    pieces_a = [(m, l) for l in range(tt // lw) for m in range(te // mc)]
    pieces_c = [(m, l) for l in range(tt // lw) for m in range(d // mc)]
    nb_rows = nk // 2
    ii_group = 2
    blocks_b = [(ig, jh, l) for l in range(tt // lw) for ig in range(te // nk // ii_group)
                for jh in range(nk // nb_rows)]

    def words(ref, *idx):
        return pltpu.bitcast(ref[idx], BF16)

    def stages(sa_ref, sb_ref, hb_out_ref, hc_ref):
        def stage_a(m, l):
            rs, ls = slice(m * mc, (m + 1) * mc), slice(l * lw, (l + 1) * lw)
            u = words(u_ref, slice(m * mc // 2, (m + 1) * mc // 2), slice(None))
            hb = words(hb_ref, slice(l * lw // 2, (l + 1) * lw // 2), slice(None))
            sa_ref[rs, ls] = lax.dot_general(u, hb, NT_DIMS, preferred_element_type=F32)

        def stage_b(ig, jh, l):
            ls = slice(l * lw, (l + 1) * lw)
            js = slice(jh * nb_rows // 2, (jh + 1) * nb_rows // 2)
            iis = range(ig * ii_group, (ig + 1) * ii_group)
            w = [jnp.zeros((nb_rows, lw), BF16) for _ in iis]
            for hd in range(PEER_HEADS):
                g2 = words(g2_ref, hd, js, ls)
                r2 = words(r2_ref, hd, js, ls)
                for t, ii in enumerate(iis):
                    n1 = pltpu.bitcast(jnp.broadcast_to(n1_ref[hd, ii:ii + 1, ls], (nb_rows // 2, lw)), BF16)
                    g1 = pltpu.bitcast(jnp.broadcast_to(g1_ref[hd, ii:ii + 1, ls], (nb_rows // 2, lw)), BF16)
                    w[t] = w[t] + jnp.where(r2 < n1, g2 * g1, jnp.zeros_like(g2))
            for t, ii in enumerate(iis):
                rs = slice(ii * nk + jh * nb_rows, ii * nk + (jh + 1) * nb_rows)
                s = sb_ref[rs, ls]
                act = 0.5 * s * (1.0 + lax.erf(s * 0.7071067811865476))
                hb_out_ref[rs, ls] = w[t] * act.astype(BF16)

        def stage_c(m, l):
            rs, ls = slice(m * mc, (m + 1) * mc), slice(l * lw, (l + 1) * lw)
            vt = words(vt_ref, slice(m * mc // 2, (m + 1) * mc // 2), slice(None))
            acc_ref[rs, ls] += jnp.dot(vt, hc_ref[:, ls], preferred_element_type=F32)

        nb = len(blocks_b) // len(pieces_a)
        for p in range(len(pieces_a)):
            for q in range(nb):
                stage_b(*blocks_b[p * nb + q])
            stage_a(*pieces_a[p])
            stage_c(*pieces_c[p])

    @pl.when(lax.rem(k, 2) == 0)
    def _():
        stages(s0_ref, s1_ref, h1_ref, h0_ref)

    @pl.when(lax.rem(k, 2) == 1)
    def _():
        stages(s1_ref, s0_ref, h0_ref, h1_ref)

    @pl.when((kc >= 0) & (ec == n_etiles - 1))
    def _():
        z = x2_ref[...] + acc_ref[...].T
        o_ref[...] = _rms(z, fnw_ref[...])


def _peer(hb, u, vt, r2, n1, g1, g2, x2, final_norm_w, *, tt, te):
    n, d = x2.shape
    ne = vt.shape[1]
    n_ttiles, n_etiles = n // tt, ne // te
    n_tiles = n_ttiles * n_etiles
    ni = te // PEER_N_KEYS
    assert ni == 8, "the first-half key rows of one expert tile must be one 32-bit sublane group"

    def tile(k, lag):
        return _divmod_nonneg(jnp.clip(k - lag, 0, n_tiles - 1), n_etiles)

    tab_spec = pl.BlockSpec((PEER_HEADS, PEER_N_KEYS // 2, tt), lambda k: (0, 0, tile(k, 1)[0]))
    row_spec = pl.BlockSpec((PEER_HEADS, ni, tt), lambda k: (0, tile(k, 1)[1], tile(k, 1)[0]))
    return pl.pallas_call(
        functools.partial(_peer_kernel, tt=tt, te=te, n_etiles=n_etiles),
        grid=(n_tiles + 2,),
        in_specs=[
            pl.BlockSpec((tt // 2, d), lambda k: (tile(k, 0)[0], 0)),
            pl.BlockSpec((te // 2, d), lambda k: (tile(k, 0)[1], 0)),
            pl.BlockSpec((d // 2, te), lambda k: (0, tile(k, 2)[1])),
            tab_spec, row_spec, row_spec, tab_spec,
            pl.BlockSpec((tt, d), lambda k: (tile(k, 2)[0], 0)),
            pl.BlockSpec((1, d), lambda k: (0, 0)),
        ],
        out_specs=pl.BlockSpec((tt, d), lambda k: (tile(k, 2)[0], 0)),
        out_shape=jax.ShapeDtypeStruct((n, d), F32),
        scratch_shapes=[
            pltpu.VMEM((te, tt), F32),
            pltpu.VMEM((te, tt), F32),
            pltpu.VMEM((te, tt), BF16),
            pltpu.VMEM((te, tt), BF16),
            pltpu.VMEM((d, tt), F32),
        ],
        compiler_params=pltpu.CompilerParams(
            dimension_semantics=("arbitrary",), vmem_limit_bytes=V7X_VMEM_LIMIT),
        name="peer",
    )(hb, u, vt, r2, n1, g1, g2, x2, final_norm_w)


def kernel(x, mem, mix_norm_w, w_in, conv_w, conv_norm_w, hgrn_lb_logits, hgrn_norm_w, w_out, xattn_norm_w,
           mem_norm_w, wq_mem, wkv_mem, wo_mem, ffn_norm_w, peer_w_query, peer_sub_keys, peer_down, peer_up,
           final_norm_w):
    b, t, d = x.shape
    depth = w_in.shape[0]
    assert depth == 1, "single-layer problem"
    assert t % 512 == 0 and d % V7X_LANES == 0
    row = lambda v: v.reshape(1, -1)
    x1 = _mixer(x, row(mix_norm_w[0]), w_in[0].astype(BF16), conv_w[0], row(conv_norm_w[0]), hgrn_lb_logits,
                row(hgrn_norm_w[0]), w_out[0].astype(BF16), tm=512)
    kv = _kv(mem, row(mem_norm_w[0]), wkv_mem[0].astype(BF16))
    x2 = _xattn(x1, row(xattn_norm_w[0]), wq_mem[0].astype(BF16), kv, wo_mem[0].astype(BF16), tm=512)
    x2 = x2.reshape(b * t, d)
    keys = peer_sub_keys[0].reshape(2 * PEER_HEADS, PEER_N_KEYS, PEER_DK).astype(BF16)
    hb, r2, n1, g1, g2 = _route(x2, row(ffn_norm_w[0]), peer_w_query[0].astype(BF16), keys, tm=512)
    u, vt = _tables(peer_down[0], peer_up[0], te=512)
    out = _peer(hb, u, vt, r2, n1, g1, g2, x2, row(final_norm_w), tt=512, te=1024)
    return out.reshape(b, t, d)
```

```python
import functools

import jax
import jax.numpy as jnp
from jax import lax
from jax.experimental import pallas as pl
from jax.experimental.pallas import tpu as pltpu

EPS = 1e-6
CONV_WIDTH = 512
HGRN_HEADS = 4
HGRN_HEAD_DIM = 128
HGRN_WIDTH = HGRN_HEADS * HGRN_HEAD_DIM
CHUNK = 64
MEM_HEADS = 4
PEER_HEADS = 8
PEER_N_KEYS = 128
PEER_DK = 128
PEER_TOPK = 16
V7X_LANES = 128
BF16_SUBLANES = 16
V7X_VMEM_LIMIT = 56 * 1024 * 1024

BF16 = jnp.bfloat16
F32 = jnp.float32
NT_DIMS = (((1,), (1,)), ((), ()))
TN_DIMS = (((0,), (0,)), ((), ()))


def _rms(x, w):
    return x * lax.rsqrt(jnp.mean(x * x, axis=-1, keepdims=True) + EPS) * w


def _sigmoid(x):
    return 1.0 / (1.0 + jnp.exp(-x))


def _mixer_kernel(x_ref, nw_ref, win_ref, convw_ref, cnw_ref, lbl_ref, hnw_ref, wout_ref, o_ref,
                  proj_ref, ohg_ref, st_ref, halo_ref, *, tm):
    cw, hw, dh = CONV_WIDTH, HGRN_WIDTH, HGRN_HEAD_DIM

    @pl.when(pl.program_id(1) == 0)
    def _():
        st_ref[...] = jnp.zeros_like(st_ref)
        halo_ref[...] = jnp.zeros_like(halo_ref)

    x = x_ref[0]
    h = _rms(x, nw_ref[...]).astype(BF16)
    proj_ref[...] = jnp.dot(h, win_ref[...], preferred_element_type=F32)

    bg = proj_ref[:, 0:cw]
    u = proj_ref[:, cw:2 * cw] * proj_ref[:, 2 * cw:3 * cw]
    row = lax.broadcasted_iota(jnp.int32, (tm, cw), 0)
    prev = halo_ref[...]
    u1 = jnp.where(row == 0, prev[7:8, :], pltpu.roll(u, 1, 0))
    u2 = jnp.where(row == 0, prev[6:7, :], jnp.where(row == 1, prev[7:8, :], pltpu.roll(u, 2, 0)))
    halo_ref[...] = u[tm - 8:tm, :]
    cwt = convw_ref[...]
    yc = _rms(bg * (u2 * cwt[0:1, :] + u1 * cwt[1:2, :] + u * cwt[2:3, :]), cnw_ref[...])

    lbl = lbl_ref[...]
    lmax = jnp.max(lbl, axis=0, keepdims=True)
    lexp = jnp.exp(lbl - lmax)
    lb = lexp[0:1, :] / jnp.sum(lexp, axis=0, keepdims=True)
    o0 = 3 * cw
    rowc = lax.broadcasted_iota(jnp.int32, (CHUNK, hw), 0)
    tril = (lax.broadcasted_iota(jnp.int32, (CHUNK, CHUNK), 0)
            >= lax.broadcasted_iota(jnp.int32, (CHUNK, CHUNK), 1))
    hcols = [slice(hd * dh, (hd + 1) * dh) for hd in range(HGRN_HEADS)]

    def chunk_local(c):
        r0 = c * CHUNK
        qc = proj_ref[r0:r0 + CHUNK, o0:o0 + hw]
        fc = proj_ref[r0:r0 + CHUNK, o0 + hw:o0 + 2 * hw]
        ic = proj_ref[r0:r0 + CHUNK, o0 + 2 * hw:o0 + 3 * hw]
        qa = qc * _sigmoid(qc)
        fg = lb + (1.0 - lb) * _sigmoid(fc)
        kk = 1.0 - fg
        g = jnp.log(fg)
        for s in (1, 2, 4, 8, 16, 32):
            g = g + jnp.where(rowc >= s, pltpu.roll(g, s, 0), 0.0)
        gl = g[CHUNK - 1:CHUNK, :]
        q_dec = (qa * jnp.exp(g)).astype(BF16)
        k_inv = (kk * jnp.exp(-g)).astype(BF16)
        k_end = (kk * jnp.exp(gl - g)).astype(BF16)
        vb = ic.astype(BF16)
        att = [lax.dot_general(q_dec[:, cs], k_inv[:, cs], NT_DIMS, preferred_element_type=F32) for cs in hcols]
        uts = [lax.dot_general(vb[:, cs], k_end[:, cs], TN_DIMS, preferred_element_type=F32) for cs in hcols]
        intra = [jnp.dot(jnp.where(tril, a, 0.0).astype(BF16), vb[:, cs], preferred_element_type=F32)
                 for a, cs in zip(att, hcols)]
        return q_dec, jnp.exp(gl), uts, intra

    nxt = chunk_local(0)
    for c in range(tm // CHUNK):
        r0 = c * CHUNK
        q_dec, dec, uts, intra = nxt
        if c + 1 < tm // CHUNK:
            nxt = chunk_local(c + 1)
        for hd, cs in enumerate(hcols):
            st = st_ref[hd]
            ohg_ref[r0:r0 + CHUNK, cs] = intra[hd] + lax.dot_general(
                q_dec[:, cs], st.astype(BF16), NT_DIMS, preferred_element_type=F32)
            st_ref[hd] = st * dec[:, cs] + uts[hd]

    gate = proj_ref[:, o0 + 3 * hw:o0 + 4 * hw]
    gate = gate * _sigmoid(gate)
    hnw = hnw_ref[...]
    ys = [yc.astype(BF16)]
    for hd in range(HGRN_HEADS):
        cs = slice(hd * dh, (hd + 1) * dh)
        oh = ohg_ref[:, cs]
        oh = oh * lax.rsqrt(jnp.mean(oh * oh, axis=-1, keepdims=True) + EPS)
        ys.append((oh * hnw[:, cs] * gate[:, cs]).astype(BF16))
    o_ref[0] = x + jnp.dot(jnp.concatenate(ys, axis=1), wout_ref[...], preferred_element_type=F32)


def _mixer(x, mix_norm_w, w_in, conv_w, conv_norm_w, lb_logits, hgrn_norm_w, w_out, *, tm):
    b, t, d = x.shape
    ncols = w_in.shape[1]
    const = lambda *_: (0, 0)
    return pl.pallas_call(
        functools.partial(_mixer_kernel, tm=tm),
        grid=(b, t // tm),
        in_specs=[
            pl.BlockSpec((1, tm, d), lambda i, j: (i, j, 0)),
            pl.BlockSpec((1, d), const),
            pl.BlockSpec((d, ncols), const),
            pl.BlockSpec(conv_w.shape, const),
            pl.BlockSpec((1, CONV_WIDTH), const),
            pl.BlockSpec(lb_logits.shape, const),
            pl.BlockSpec((1, HGRN_WIDTH), const),
            pl.BlockSpec((d, d), const),
        ],
        out_specs=pl.BlockSpec((1, tm, d), lambda i, j: (i, j, 0)),
        out_shape=jax.ShapeDtypeStruct((b, t, d), F32),
        scratch_shapes=[
            pltpu.VMEM((tm, ncols), F32),
            pltpu.VMEM((tm, HGRN_WIDTH), F32),
            pltpu.VMEM((HGRN_HEADS, HGRN_HEAD_DIM, HGRN_HEAD_DIM), F32),
            pltpu.VMEM((8, CONV_WIDTH), F32),
        ],
        compiler_params=pltpu.CompilerParams(
            dimension_semantics=("arbitrary", "arbitrary"), vmem_limit_bytes=V7X_VMEM_LIMIT),
        name="mixer",
    )(x, mix_norm_w, w_in, conv_w, conv_norm_w, lb_logits, hgrn_norm_w, w_out)


def _kv_kernel(mem_ref, nw_ref, wkv_ref, o_ref):
    mn = _rms(mem_ref[0], nw_ref[...]).astype(BF16)
    o_ref[0] = jnp.dot(mn, wkv_ref[...], preferred_element_type=F32).astype(BF16)


def _kv(mem, mem_norm_w, wkv):
    b, m, d = mem.shape
    return pl.pallas_call(
        _kv_kernel,
        grid=(b,),
        in_specs=[
            pl.BlockSpec((1, m, d), lambda i: (i, 0, 0)),
            pl.BlockSpec((1, d), lambda i: (0, 0)),
            pl.BlockSpec((d, 2 * d), lambda i: (0, 0)),
        ],
        out_specs=pl.BlockSpec((1, m, 2 * d), lambda i: (i, 0, 0)),
        out_shape=jax.ShapeDtypeStruct((b, m, 2 * d), BF16),
        compiler_params=pltpu.CompilerParams(
            dimension_semantics=("arbitrary",), vmem_limit_bytes=V7X_VMEM_LIMIT),
        name="kv",
    )(mem, mem_norm_w, wkv)


def _xattn_kernel(x_ref, nw_ref, wq_ref, kv_ref, wo_ref, o_ref):
    x = x_ref[0]
    d = x.shape[-1]
    hd_dim = d // MEM_HEADS
    h = _rms(x, nw_ref[...]).astype(BF16)
    q = jnp.dot(h, wq_ref[...], preferred_element_type=F32)
    heads = [slice(hd * hd_dim, (hd + 1) * hd_dim) for hd in range(MEM_HEADS)]
    scores = [lax.dot_general(q[:, cs].astype(BF16), kv_ref[0, :, cs], NT_DIMS, preferred_element_type=F32)
              for cs in heads]
    outs = []
    for hd, cs in enumerate(heads):
        vh = kv_ref[0, :, d + hd * hd_dim:d + (hd + 1) * hd_dim]
        s = scores[hd] * (hd_dim ** -0.5)
        p = jnp.exp(s - jnp.max(s, axis=-1, keepdims=True))
        p = p / jnp.sum(p, axis=-1, keepdims=True)
        outs.append(jnp.dot(p.astype(BF16), vh, preferred_element_type=F32).astype(BF16))
    o_ref[0] = x + jnp.dot(jnp.concatenate(outs, axis=1), wo_ref[...], preferred_element_type=F32)


def _xattn(x, xattn_norm_w, wq, kv, wo, *, tm):
    b, t, d = x.shape
    m = kv.shape[1]
    const = lambda *_: (0, 0)
    return pl.pallas_call(
        _xattn_kernel,
        grid=(b, t // tm),
        in_specs=[
            pl.BlockSpec((1, tm, d), lambda i, j: (i, j, 0)),
            pl.BlockSpec((1, d), const),
            pl.BlockSpec((d, d), const),
            pl.BlockSpec((1, m, 2 * d), lambda i, j: (i, 0, 0)),
            pl.BlockSpec((d, d), const),
        ],
        out_specs=pl.BlockSpec((1, tm, d), lambda i, j: (i, j, 0)),
        out_shape=jax.ShapeDtypeStruct((b, t, d), F32),
        compiler_params=pltpu.CompilerParams(
            dimension_semantics=("arbitrary", "arbitrary"), vmem_limit_bytes=V7X_VMEM_LIMIT),
        name="xattn",
    )(x, xattn_norm_w, wq, kv, wo)


def _staircase():
    return [(q, PEER_TOPK // (q + 1)) for q in range(PEER_TOPK)]


N_CAND = sum(n for _, n in _staircase())
N_CAND_PAD = -(-N_CAND // 8) * 8


def _sorting_network(n):
    pairs = []

    def merge(lo, hi, r):
        step = r * 2
        if step < hi - lo:
            merge(lo, hi, step)
            merge(lo + r, hi, step)
            pairs.extend((i, i + r) for i in range(lo + r, hi - r, step))
        else:
            pairs.append((lo, lo + r))

    def sort(lo, hi):
        if hi > lo:
            mid = lo + (hi - lo) // 2
            sort(lo, mid)
            sort(mid + 1, hi)
            merge(lo, hi, 1)

    sort(0, n - 1)
    return pairs


def _top_values_sorted(rows, count, store):
    rows = list(rows)
    while len(rows) & (len(rows) - 1):
        rows.append(jnp.full_like(rows[0], -jnp.inf))
    for i, j in _sorting_network(len(rows)):
        rows[i], rows[j] = jnp.maximum(rows[i], rows[j]), jnp.minimum(rows[i], rows[j])
    m = None
    for r in range(count):
        m = jnp.max(rows[0], axis=0, keepdims=True)
        store(r, m)
        eq = rows[0] == m
        keep = min(len(rows), count - r - 1)
        for q in range(keep):
            nxt = rows[q + 1] if q + 1 < len(rows) else jnp.full_like(rows[q], -jnp.inf)
            rows[q] = jnp.where(eq, nxt, rows[q])
        rows = rows[:keep]
    return m


def _row_blocks(x):
    return [x[i:i + 8, :] for i in range(0, x.shape[0], 8)]


def _select(s1, s2, a_ref, b_ref, cand_ref, ls):
    def store_a(r, m):
        a_ref[r:r + 1, ls] = m

    def store_b(r, m):
        b_ref[r:r + 1, ls] = m

    _top_values_sorted(_row_blocks(s1), PEER_TOPK, store_a)
    _top_values_sorted(_row_blocks(s2), PEER_TOPK, store_b)
    rank2 = jnp.zeros(s2.shape, F32)
    for r in range(PEER_TOPK):
        rank2 = jnp.where(s2 < b_ref[r:r + 1, ls], float(r + 1), rank2)
    a = a_ref[:, ls]
    off = 0
    for qq, n in _staircase():
        cand_ref[off:off + n, ls] = a[0:n, :] + b_ref[qq:qq + 1, ls]
        off += n
    if N_CAND_PAD > N_CAND:
        cand_ref[N_CAND:N_CAND_PAD, ls] = jnp.full((N_CAND_PAD - N_CAND, a.shape[1]), -jnp.inf, F32)
    cand = cand_ref[:, ls]
    tau = _top_values_sorted(_row_blocks(cand), PEER_TOPK, lambda r, m: None)
    z = jnp.sum(jnp.where(cand >= tau, jnp.exp(cand - cand[0:1, :]), 0.0), axis=0, keepdims=True)
    n1 = jnp.zeros(s1.shape, F32)
    for qq in range(PEER_TOPK):
        ok = a + b_ref[qq:qq + 1, ls] >= tau
        theta = jnp.min(jnp.where(ok, a, jnp.inf), axis=0, keepdims=True)
        n1 = jnp.where(s1 >= theta, float(qq + 1), n1)
    g1 = jnp.exp(s1 - a[0:1, :])
    g2 = jnp.exp(s2 - b_ref[0:1, ls]) * (1.0 / z)
    return rank2, n1, g1, g2


def _bf16_bits(x):
    return pltpu.bitcast(x.astype(BF16).astype(F32), jnp.uint32)


def _dup_bf16_words(x):
    bits = _bf16_bits(x)
    return bits | lax.shift_right_logical(bits, jnp.uint32(16))


def _store_row_pairs(dst, x, slab_ref):
    rows, cols = x.shape
    for c in range(cols // V7X_LANES):
        cs = slice(c * V7X_LANES, (c + 1) * V7X_LANES)
        slab_ref[c] = x[:, cs]
        even = slab_ref[c, pl.ds(0, rows // 2, stride=2), :]
        odd = slab_ref[c, pl.ds(1, rows // 2, stride=2), :]
        dst[:, cs] = lax.shift_right_logical(_bf16_bits(even), jnp.uint32(16)) | _bf16_bits(odd)


def _route_kernel(x_ref, nw_ref, wqry_ref, keys_ref, hb_ref, r2_ref, n1_ref, g1_ref, g2_ref,
                  a_ref, b_ref, cand_ref, hslab_ref, kslab_ref, s1_ref, s2_ref):
    hf = _rms(x_ref[...], nw_ref[...])
    tm = hf.shape[0]
    _store_row_pairs(hb_ref, hf, hslab_ref)
    q = jnp.dot(hf.astype(BF16), wqry_ref[...], preferred_element_type=F32).astype(BF16)
    for hd in range(PEER_HEADS):
        c1 = (2 * hd) * PEER_DK
        c2 = (2 * hd + 1) * PEER_DK
        s1_ref[...] = lax.dot_general(keys_ref[2 * hd], q[:, c1:c1 + PEER_DK], NT_DIMS,
                                      preferred_element_type=F32)
        s2_ref[...] = lax.dot_general(keys_ref[2 * hd + 1], q[:, c2:c2 + PEER_DK], NT_DIMS,
                                      preferred_element_type=F32)
        for lt in range(tm // V7X_LANES):
            ls = slice(lt * V7X_LANES, (lt + 1) * V7X_LANES)
            rank2, n1, g1, g2 = _select(s1_ref[:, ls], s2_ref[:, ls], a_ref, b_ref, cand_ref, ls)
            _store_row_pairs(r2_ref.at[hd, :, ls], rank2, kslab_ref.at[pl.ds(2 * lt, 1)])
            n1_ref[hd, :, ls] = _dup_bf16_words(n1)
            g1_ref[hd, :, ls] = _dup_bf16_words(g1)
            _store_row_pairs(g2_ref.at[hd, :, ls], g2, kslab_ref.at[pl.ds(2 * lt + 1, 1)])


def _route(x2, ffn_norm_w, w_query, keys, *, tm):
    n, d = x2.shape
    nq = w_query.shape[1]
    tab = jax.ShapeDtypeStruct((PEER_HEADS, PEER_N_KEYS, n), jnp.uint32)
    tabh = jax.ShapeDtypeStruct((PEER_HEADS, PEER_N_KEYS // 2, n), jnp.uint32)
    tab_spec = pl.BlockSpec((PEER_HEADS, PEER_N_KEYS, tm), lambda i: (0, 0, i))
    tabh_spec = pl.BlockSpec((PEER_HEADS, PEER_N_KEYS // 2, tm), lambda i: (0, 0, i))
    return pl.pallas_call(
        _route_kernel,
        grid=(n // tm,),
        in_specs=[
            pl.BlockSpec((tm, d), lambda i: (i, 0)),
            pl.BlockSpec((1, d), lambda i: (0, 0)),
            pl.BlockSpec((d, nq), lambda i: (0, 0)),
            pl.BlockSpec(keys.shape, lambda i: (0, 0, 0)),
        ],
        out_specs=[pl.BlockSpec((tm // 2, d), lambda i: (i, 0)), tabh_spec, tab_spec, tab_spec, tabh_spec],
        out_shape=[jax.ShapeDtypeStruct((n // 2, d), jnp.uint32), tabh, tab, tab, tabh],
        scratch_shapes=[
            pltpu.VMEM((PEER_TOPK, tm), F32),
            pltpu.VMEM((PEER_TOPK, tm), F32),
            pltpu.VMEM((N_CAND_PAD, tm), F32),
            pltpu.VMEM((d // V7X_LANES, tm, V7X_LANES), F32),
            pltpu.VMEM((2 * tm // V7X_LANES, PEER_N_KEYS, V7X_LANES), F32),
            pltpu.VMEM((PEER_N_KEYS, tm), F32),
            pltpu.VMEM((PEER_N_KEYS, tm), F32),
        ],
        compiler_params=pltpu.CompilerParams(
            dimension_semantics=("arbitrary",), vmem_limit_bytes=V7X_VMEM_LIMIT),
        name="route",
    )(x2, ffn_norm_w, w_query, keys)


def _tables_kernel(down_ref, up_ref, u_ref, vt_ref, uslab_ref, vslab_ref):
    _store_row_pairs(u_ref, down_ref[...], uslab_ref)
    _store_row_pairs(vt_ref, up_ref[...].T, vslab_ref)


def _tables(peer_down, peer_up, *, te):
    ne, d = peer_down.shape
    return pl.pallas_call(
        _tables_kernel,
        grid=(ne // te,),
        in_specs=[pl.BlockSpec((te, d), lambda i: (i, 0)), pl.BlockSpec((te, d), lambda i: (i, 0))],
        out_specs=[pl.BlockSpec((te // 2, d), lambda i: (i, 0)), pl.BlockSpec((d // 2, te), lambda i: (0, i))],
        out_shape=[jax.ShapeDtypeStruct((ne // 2, d), jnp.uint32), jax.ShapeDtypeStruct((d // 2, ne), jnp.uint32)],
        scratch_shapes=[pltpu.VMEM((d // V7X_LANES, te, V7X_LANES), F32),
                        pltpu.VMEM((te // V7X_LANES, d, V7X_LANES), F32)],
        compiler_params=pltpu.CompilerParams(
            dimension_semantics=("arbitrary",), vmem_limit_bytes=V7X_VMEM_LIMIT),
        name="tables",
    )(peer_down, peer_up)


def _divmod_nonneg(t, n):
    if n & (n - 1) == 0:
        return lax.shift_right_logical(t, n.bit_length() - 1), t & (n - 1)
    return t // n, lax.rem(t, n)


def _peer_kernel(hb_ref, u_ref, vt_ref, r2_ref, n1_ref, g1_ref, g2_ref, x2_ref, fnw_ref, o_ref,
                 s0_ref, s1_ref, h0_ref, h1_ref, acc_ref, *, tt, te, n_etiles):
    k = pl.program_id(0)
    kc = k - 2
    ec = _divmod_nonneg(jnp.maximum(kc, 0), n_etiles)[1]
    nk = PEER_N_KEYS
    lw = 2 * V7X_LANES

    @pl.when(k == 0)
    def _():
        for ref in (s0_ref, s1_ref, h0_ref, h1_ref):
            ref[...] = jnp.zeros_like(ref)

    @pl.when((k == 0) | ((kc >= 0) & (ec == 0)))
    def _():
        acc_ref[...] = jnp.zeros_like(acc_ref)

    mc = 2 * nk
    d = acc_ref.shape[0]
    pieces_a = [(m, l) for l in range(tt // lw) for m in range(te // mc)]
    kc = 8 * nk
    pieces_c = [(m, l, kk) for l in range(tt // lw) for kk in range(te // kc) for m in range(d // mc)]
    nb_rows = nk // 2
    ii_group = 2
    blocks_b = [(ig, jh, l) for l in range(tt // lw) for ig in range(te // nk // ii_group)
                for jh in range(nk // nb_rows)]

    def words(ref, *idx):
        return pltpu.bitcast(ref[idx], BF16)

    def stages(sa_ref, sb_ref, hb_out_ref, hc_ref):
        def stage_a(m, l):
            rs, ls = slice(m * mc, (m + 1) * mc), slice(l * lw, (l + 1) * lw)
            u = words(u_ref, slice(m * mc // 2, (m + 1) * mc // 2), slice(None))
            hb = words(hb_ref, slice(l * lw // 2, (l + 1) * lw // 2), slice(None))
            sa_ref[rs, ls] = lax.dot_general(u, hb, NT_DIMS, preferred_element_type=F32)

        def stage_b(ig, jh, l):
            ls = slice(l * lw, (l + 1) * lw)
            js = slice(jh * nb_rows // 2, (jh + 1) * nb_rows // 2)
            iis = range(ig * ii_group, (ig + 1) * ii_group)
            w = [jnp.zeros((nb_rows, lw), BF16) for _ in iis]
            for hd in range(PEER_HEADS):
                g2 = words(g2_ref, hd, js, ls)
                r2 = words(r2_ref, hd, js, ls)
                for t, ii in enumerate(iis):
                    n1 = pltpu.bitcast(jnp.broadcast_to(n1_ref[hd, ii:ii + 1, ls], (nb_rows // 2, lw)), BF16)
                    g1 = pltpu.bitcast(jnp.broadcast_to(g1_ref[hd, ii:ii + 1, ls], (nb_rows // 2, lw)), BF16)
                    w[t] = w[t] + jnp.where(r2 < n1, g2 * g1, jnp.zeros_like(g2))
            for t, ii in enumerate(iis):
                rs = slice(ii * nk + jh * nb_rows, ii * nk + (jh + 1) * nb_rows)
                s = sb_ref[rs, ls]
                act = 0.5 * s * (1.0 + lax.erf(s * 0.7071067811865476))
                hb_out_ref[rs, ls] = w[t] * act.astype(BF16)

        def stage_c(m, l, kk):
            rs, ls, ks = slice(m * mc, (m + 1) * mc), slice(l * lw, (l + 1) * lw), slice(kk * kc, (kk + 1) * kc)
            vt = words(vt_ref, slice(m * mc // 2, (m + 1) * mc // 2), ks)
            acc_ref[rs, ls] += jnp.dot(vt, hc_ref[ks, ls], preferred_element_type=F32)

        nb = len(blocks_b) // len(pieces_a)
        for p in range(len(pieces_a)):
            for q in range(nb):
                stage_b(*blocks_b[p * nb + q])
            stage_a(*pieces_a[p])
            stage_c(*pieces_c[p])

    @pl.when(lax.rem(k, 2) == 0)
    def _():
        stages(s0_ref, s1_ref, h1_ref, h0_ref)

    @pl.when(lax.rem(k, 2) == 1)
    def _():
        stages(s1_ref, s0_ref, h0_ref, h1_ref)

    @pl.when((kc >= 0) & (ec == n_etiles - 1))
    def _():
        z = x2_ref[...] + acc_ref[...].T
        o_ref[...] = _rms(z, fnw_ref[...])


def _peer(hb, u, vt, r2, n1, g1, g2, x2, final_norm_w, *, tt, te):
    n, d = x2.shape
    ne = vt.shape[1]
    n_ttiles, n_etiles = n // tt, ne // te
    n_tiles = n_ttiles * n_etiles
    ni = te // PEER_N_KEYS
    assert ni % 8 == 0, "the first-half key rows of one expert tile must be whole 32-bit sublane groups"

    def tile(k, lag):
        return _divmod_nonneg(jnp.clip(k - lag, 0, n_tiles - 1), n_etiles)

    tab_spec = pl.BlockSpec((PEER_HEADS, PEER_N_KEYS // 2, tt), lambda k: (0, 0, tile(k, 1)[0]))
    row_spec = pl.BlockSpec((PEER_HEADS, ni, tt), lambda k: (0, tile(k, 1)[1], tile(k, 1)[0]))
    return pl.pallas_call(
        functools.partial(_peer_kernel, tt=tt, te=te, n_etiles=n_etiles),
        grid=(n_tiles + 2,),
        in_specs=[
            pl.BlockSpec((tt // 2, d), lambda k: (tile(k, 0)[0], 0)),
            pl.BlockSpec((te // 2, d), lambda k: (tile(k, 0)[1], 0)),
            pl.BlockSpec((d // 2, te), lambda k: (0, tile(k, 2)[1])),
            tab_spec, row_spec, row_spec, tab_spec,
            pl.BlockSpec((tt, d), lambda k: (tile(k, 2)[0], 0)),
            pl.BlockSpec((1, d), lambda k: (0, 0)),
        ],
        out_specs=pl.BlockSpec((tt, d), lambda k: (tile(k, 2)[0], 0)),
        out_shape=jax.ShapeDtypeStruct((n, d), F32),
        scratch_shapes=[
            pltpu.VMEM((te, tt), F32),
            pltpu.VMEM((te, tt), F32),
            pltpu.VMEM((te, tt), BF16),
            pltpu.VMEM((te, tt), BF16),
            pltpu.VMEM((d, tt), F32),
        ],
        compiler_params=pltpu.CompilerParams(
            dimension_semantics=("arbitrary",), vmem_limit_bytes=V7X_VMEM_LIMIT),
        name="peer",
    )(hb, u, vt, r2, n1, g1, g2, x2, final_norm_w)


def kernel(x, mem, mix_norm_w, w_in, conv_w, conv_norm_w, hgrn_lb_logits, hgrn_norm_w, w_out, xattn_norm_w,
           mem_norm_w, wq_mem, wkv_mem, wo_mem, ffn_norm_w, peer_w_query, peer_sub_keys, peer_down, peer_up,
           final_norm_w):
    b, t, d = x.shape
    depth = w_in.shape[0]
    assert depth == 1, "single-layer problem"
    assert t % 512 == 0 and d % V7X_LANES == 0
    row = lambda v: v.reshape(1, -1)
    x1 = _mixer(x, row(mix_norm_w[0]), w_in[0].astype(BF16), conv_w[0], row(conv_norm_w[0]), hgrn_lb_logits,
                row(hgrn_norm_w[0]), w_out[0].astype(BF16), tm=512)
    kv = _kv(mem, row(mem_norm_w[0]), wkv_mem[0].astype(BF16))
    x2 = _xattn(x1, row(xattn_norm_w[0]), wq_mem[0].astype(BF16), kv, wo_mem[0].astype(BF16), tm=512)
    x2 = x2.reshape(b * t, d)
    keys = peer_sub_keys[0].reshape(2 * PEER_HEADS, PEER_N_KEYS, PEER_DK).astype(BF16)
    hb, r2, n1, g1, g2 = _route(x2, row(ffn_norm_w[0]), peer_w_query[0].astype(BF16), keys, tm=512)
    u, vt = _tables(peer_down[0], peer_up[0], te=512)
    out = _peer(hb, u, vt, r2, n1, g1, g2, x2, row(final_norm_w), tt=512, te=2048)
    return out.reshape(b, t, d)
```

```python
import functools

import jax
import jax.numpy as jnp
from jax import lax
from jax.experimental import pallas as pl
from jax.experimental.pallas import tpu as pltpu

EPS = 1e-6
CONV_WIDTH = 512
HGRN_HEADS = 4
HGRN_HEAD_DIM = 128
HGRN_WIDTH = HGRN_HEADS * HGRN_HEAD_DIM
CHUNK = 64
MEM_HEADS = 4
PEER_HEADS = 8
PEER_N_KEYS = 128
PEER_DK = 128
PEER_TOPK = 16
V7X_LANES = 128
V7X_SUBLANES = 8
V7X_VMEM_LIMIT = 56 * 1024 * 1024
TOKEN_TILE = 512
EXPERT_TILE = 2048
TABLE_TILE = 512

BF16 = jnp.bfloat16
F32 = jnp.float32
NT_DIMS = (((1,), (1,)), ((), ()))
TN_DIMS = (((0,), (0,)), ((), ()))


def _rms(x, w):
    return x * lax.rsqrt(jnp.mean(x * x, axis=-1, keepdims=True) + EPS) * w


def _sigmoid(x):
    return 1.0 / (1.0 + jnp.exp(-x))


def _mixer_kernel(x_ref, nw_ref, win_ref, convw_ref, cnw_ref, lbl_ref, hnw_ref, wout_ref, o_ref,
                  proj_ref, ohg_ref, st_ref, halo_ref, *, tm):
    cw, hw, dh = CONV_WIDTH, HGRN_WIDTH, HGRN_HEAD_DIM

    @pl.when(pl.program_id(1) == 0)
    def _():
        st_ref[...] = jnp.zeros_like(st_ref)
        halo_ref[...] = jnp.zeros_like(halo_ref)

    x = x_ref[0]
    h = _rms(x, nw_ref[...]).astype(BF16)
    proj_ref[...] = jnp.dot(h, win_ref[...], preferred_element_type=F32)

    bg = proj_ref[:, 0:cw]
    u = proj_ref[:, cw:2 * cw] * proj_ref[:, 2 * cw:3 * cw]
    row = lax.broadcasted_iota(jnp.int32, (tm, cw), 0)
    prev = halo_ref[...]
    u1 = jnp.where(row == 0, prev[7:8, :], pltpu.roll(u, 1, 0))
    u2 = jnp.where(row == 0, prev[6:7, :], jnp.where(row == 1, prev[7:8, :], pltpu.roll(u, 2, 0)))
    halo_ref[...] = u[tm - 8:tm, :]
    cwt = convw_ref[...]
    yc = _rms(bg * (u2 * cwt[0:1, :] + u1 * cwt[1:2, :] + u * cwt[2:3, :]), cnw_ref[...])

    lbl = lbl_ref[...]
    lmax = jnp.max(lbl, axis=0, keepdims=True)
    lexp = jnp.exp(lbl - lmax)
    lb = lexp[0:1, :] / jnp.sum(lexp, axis=0, keepdims=True)
    o0 = 3 * cw
    rowc = lax.broadcasted_iota(jnp.int32, (CHUNK, hw), 0)
    tril = (lax.broadcasted_iota(jnp.int32, (CHUNK, CHUNK), 0)
            >= lax.broadcasted_iota(jnp.int32, (CHUNK, CHUNK), 1))
    hcols = [slice(hd * dh, (hd + 1) * dh) for hd in range(HGRN_HEADS)]

    def chunk_local(c):
        r0 = c * CHUNK
        qc = proj_ref[r0:r0 + CHUNK, o0:o0 + hw]
        fc = proj_ref[r0:r0 + CHUNK, o0 + hw:o0 + 2 * hw]
        ic = proj_ref[r0:r0 + CHUNK, o0 + 2 * hw:o0 + 3 * hw]
        qa = qc * _sigmoid(qc)
        fg = lb + (1.0 - lb) * _sigmoid(fc)
        kk = 1.0 - fg
        g = jnp.log(fg)
        for s in (1, 2, 4, 8, 16, 32):
            g = g + jnp.where(rowc >= s, pltpu.roll(g, s, 0), 0.0)
        gl = g[CHUNK - 1:CHUNK, :]
        q_dec = (qa * jnp.exp(g)).astype(BF16)
        k_inv = (kk * jnp.exp(-g)).astype(BF16)
        k_end = (kk * jnp.exp(gl - g)).astype(BF16)
        vb = ic.astype(BF16)
        att = [lax.dot_general(q_dec[:, cs], k_inv[:, cs], NT_DIMS, preferred_element_type=F32) for cs in hcols]
        uts = [lax.dot_general(vb[:, cs], k_end[:, cs], TN_DIMS, preferred_element_type=F32) for cs in hcols]
        intra = [jnp.dot(jnp.where(tril, a, 0.0).astype(BF16), vb[:, cs], preferred_element_type=F32)
                 for a, cs in zip(att, hcols)]
        return q_dec, jnp.exp(gl), uts, intra

    nxt = chunk_local(0)
    for c in range(tm // CHUNK):
        r0 = c * CHUNK
        q_dec, dec, uts, intra = nxt
        if c + 1 < tm // CHUNK:
            nxt = chunk_local(c + 1)
        for hd, cs in enumerate(hcols):
            st = st_ref[hd]
            ohg_ref[r0:r0 + CHUNK, cs] = intra[hd] + lax.dot_general(
                q_dec[:, cs], st.astype(BF16), NT_DIMS, preferred_element_type=F32)
            st_ref[hd] = st * dec[:, cs] + uts[hd]

    gate = proj_ref[:, o0 + 3 * hw:o0 + 4 * hw]
    gate = gate * _sigmoid(gate)
    hnw = hnw_ref[...]
    ys = [yc.astype(BF16)]
    for hd in range(HGRN_HEADS):
        cs = slice(hd * dh, (hd + 1) * dh)
        oh = ohg_ref[:, cs]
        oh = oh * lax.rsqrt(jnp.mean(oh * oh, axis=-1, keepdims=True) + EPS)
        ys.append((oh * hnw[:, cs] * gate[:, cs]).astype(BF16))
    o_ref[0] = x + jnp.dot(jnp.concatenate(ys, axis=1), wout_ref[...], preferred_element_type=F32)


def _mixer(x, mix_norm_w, w_in, conv_w, conv_norm_w, lb_logits, hgrn_norm_w, w_out, *, tm):
    b, t, d = x.shape
    ncols = w_in.shape[1]
    const = lambda *_: (0, 0)
    return pl.pallas_call(
        functools.partial(_mixer_kernel, tm=tm),
        grid=(b, t // tm),
        in_specs=[
            pl.BlockSpec((1, tm, d), lambda i, j: (i, j, 0)),
            pl.BlockSpec((1, d), const),
            pl.BlockSpec((d, ncols), const),
            pl.BlockSpec(conv_w.shape, const),
            pl.BlockSpec((1, CONV_WIDTH), const),
            pl.BlockSpec(lb_logits.shape, const),
            pl.BlockSpec((1, HGRN_WIDTH), const),
            pl.BlockSpec((d, d), const),
        ],
        out_specs=pl.BlockSpec((1, tm, d), lambda i, j: (i, j, 0)),
        out_shape=jax.ShapeDtypeStruct((b, t, d), F32),
        scratch_shapes=[
            pltpu.VMEM((tm, ncols), F32),
            pltpu.VMEM((tm, HGRN_WIDTH), F32),
            pltpu.VMEM((HGRN_HEADS, HGRN_HEAD_DIM, HGRN_HEAD_DIM), F32),
            pltpu.VMEM((8, CONV_WIDTH), F32),
        ],
        compiler_params=pltpu.CompilerParams(
            dimension_semantics=("arbitrary", "arbitrary"), vmem_limit_bytes=V7X_VMEM_LIMIT),
        name="mixer",
    )(x, mix_norm_w, w_in, conv_w, conv_norm_w, lb_logits, hgrn_norm_w, w_out)


def _kv_kernel(mem_ref, nw_ref, wkv_ref, o_ref):
    mn = _rms(mem_ref[0], nw_ref[...]).astype(BF16)
    o_ref[0] = jnp.dot(mn, wkv_ref[...], preferred_element_type=F32).astype(BF16)


def _kv(mem, mem_norm_w, wkv):
    b, m, d = mem.shape
    return pl.pallas_call(
        _kv_kernel,
        grid=(b,),
        in_specs=[
            pl.BlockSpec((1, m, d), lambda i: (i, 0, 0)),
            pl.BlockSpec((1, d), lambda i: (0, 0)),
            pl.BlockSpec((d, 2 * d), lambda i: (0, 0)),
        ],
        out_specs=pl.BlockSpec((1, m, 2 * d), lambda i: (i, 0, 0)),
        out_shape=jax.ShapeDtypeStruct((b, m, 2 * d), BF16),
        compiler_params=pltpu.CompilerParams(
            dimension_semantics=("arbitrary",), vmem_limit_bytes=V7X_VMEM_LIMIT),
        name="kv",
    )(mem, mem_norm_w, wkv)


def _xattn_kernel(x_ref, nw_ref, wq_ref, kv_ref, wo_ref, o_ref):
    x = x_ref[0]
    d = x.shape[-1]
    hd_dim = d // MEM_HEADS
    h = _rms(x, nw_ref[...]).astype(BF16)
    q = jnp.dot(h, wq_ref[...], preferred_element_type=F32)
    heads = [slice(hd * hd_dim, (hd + 1) * hd_dim) for hd in range(MEM_HEADS)]
    scores = [lax.dot_general(q[:, cs].astype(BF16), kv_ref[0, :, cs], NT_DIMS, preferred_element_type=F32)
              for cs in heads]
    outs = []
    for hd, cs in enumerate(heads):
        vh = kv_ref[0, :, d + hd * hd_dim:d + (hd + 1) * hd_dim]
        s = scores[hd] * (hd_dim ** -0.5)
        p = jnp.exp(s - jnp.max(s, axis=-1, keepdims=True))
        p = p / jnp.sum(p, axis=-1, keepdims=True)
        outs.append(jnp.dot(p.astype(BF16), vh, preferred_element_type=F32).astype(BF16))
    o_ref[0] = x + jnp.dot(jnp.concatenate(outs, axis=1), wo_ref[...], preferred_element_type=F32)


def _xattn(x, xattn_norm_w, wq, kv, wo, *, tm):
    b, t, d = x.shape
    m = kv.shape[1]
    const = lambda *_: (0, 0)
    return pl.pallas_call(
        _xattn_kernel,
        grid=(b, t // tm),
        in_specs=[
            pl.BlockSpec((1, tm, d), lambda i, j: (i, j, 0)),
            pl.BlockSpec((1, d), const),
            pl.BlockSpec((d, d), const),
            pl.BlockSpec((1, m, 2 * d), lambda i, j: (i, 0, 0)),
            pl.BlockSpec((d, d), const),
        ],
        out_specs=pl.BlockSpec((1, tm, d), lambda i, j: (i, j, 0)),
        out_shape=jax.ShapeDtypeStruct((b, t, d), F32),
        compiler_params=pltpu.CompilerParams(
            dimension_semantics=("arbitrary", "arbitrary"), vmem_limit_bytes=V7X_VMEM_LIMIT),
        name="xattn",
    )(x, xattn_norm_w, wq, kv, wo)


def _mix_attn_kernel(x_ref, mnw_ref, win_ref, convw_ref, cnw_ref, lbl_ref, hnw_ref, wout_ref,
                     xnw_ref, wq_ref, kv_ref, wo_ref, o_ref, proj_ref, ohg_ref, st_ref, halo_ref, x1_ref, *, tm):
    _mixer_kernel(x_ref, mnw_ref, win_ref, convw_ref, cnw_ref, lbl_ref, hnw_ref, wout_ref, x1_ref,
                  proj_ref, ohg_ref, st_ref, halo_ref, tm=tm)
    _xattn_kernel(x1_ref, xnw_ref, wq_ref, kv_ref, wo_ref, o_ref)


def _mix_attn(x, mix_norm_w, w_in, conv_w, conv_norm_w, lb_logits, hgrn_norm_w, w_out, xattn_norm_w, wq, kv, wo,
              *, tm):
    b, t, d = x.shape
    ncols = w_in.shape[1]
    m = kv.shape[1]
    const = lambda *_: (0, 0)
    tile = pl.BlockSpec((1, tm, d), lambda i, j: (i, j, 0))
    return pl.pallas_call(
        functools.partial(_mix_attn_kernel, tm=tm),
        grid=(b, t // tm),
        in_specs=[
            tile,
            pl.BlockSpec((1, d), const),
            pl.BlockSpec((d, ncols), const),
            pl.BlockSpec(conv_w.shape, const),
            pl.BlockSpec((1, CONV_WIDTH), const),
            pl.BlockSpec(lb_logits.shape, const),
            pl.BlockSpec((1, HGRN_WIDTH), const),
            pl.BlockSpec((d, d), const),
            pl.BlockSpec((1, d), const),
            pl.BlockSpec((d, d), const),
            pl.BlockSpec((1, m, 2 * d), lambda i, j: (i, 0, 0)),
            pl.BlockSpec((d, d), const),
        ],
        out_specs=tile,
        out_shape=jax.ShapeDtypeStruct((b, t, d), F32),
        scratch_shapes=[
            pltpu.VMEM((tm, ncols), F32),
            pltpu.VMEM((tm, HGRN_WIDTH), F32),
            pltpu.VMEM((HGRN_HEADS, HGRN_HEAD_DIM, HGRN_HEAD_DIM), F32),
            pltpu.VMEM((8, CONV_WIDTH), F32),
            pltpu.VMEM((1, tm, d), F32),
        ],
        compiler_params=pltpu.CompilerParams(
            dimension_semantics=("arbitrary", "arbitrary"), vmem_limit_bytes=V7X_VMEM_LIMIT),
        name="mix_attn",
    )(x, mix_norm_w, w_in, conv_w, conv_norm_w, lb_logits, hgrn_norm_w, w_out, xattn_norm_w, wq, kv, wo)


def _staircase():
    return [(q, PEER_TOPK // (q + 1)) for q in range(PEER_TOPK)]


N_CAND = sum(n for _, n in _staircase())
N_CAND_PAD = -(-N_CAND // 8) * 8


def _sorting_network(n):
    pairs = []

    def merge(lo, hi, r):
        step = r * 2
        if step < hi - lo:
            merge(lo, hi, step)
            merge(lo + r, hi, step)
            pairs.extend((i, i + r) for i in range(lo + r, hi - r, step))
        else:
            pairs.append((lo, lo + r))

    def sort(lo, hi):
        if hi > lo:
            mid = lo + (hi - lo) // 2
            sort(lo, mid)
            sort(mid + 1, hi)
            merge(lo, hi, 1)

    sort(0, n - 1)
    return pairs


def _top_values_sorted(rows, count, store):
    rows = list(rows)
    while len(rows) & (len(rows) - 1):
        rows.append(jnp.full_like(rows[0], -jnp.inf))
    for i, j in _sorting_network(len(rows)):
        rows[i], rows[j] = jnp.maximum(rows[i], rows[j]), jnp.minimum(rows[i], rows[j])
    m = None
    for r in range(count):
        m = jnp.max(rows[0], axis=0, keepdims=True)
        store(r, m)
        eq = rows[0] == m
        keep = min(len(rows), count - r - 1)
        for q in range(keep):
            nxt = rows[q + 1] if q + 1 < len(rows) else jnp.full_like(rows[q], -jnp.inf)
            rows[q] = jnp.where(eq, nxt, rows[q])
        rows = rows[:keep]
    return m


def _row_blocks(x):
    return [x[i:i + V7X_SUBLANES, :] for i in range(0, x.shape[0], V7X_SUBLANES)]


def _select(s1, s2, a_ref, b_ref, cand_ref, ls):
    def store_a(r, m):
        a_ref[r:r + 1, ls] = m

    def store_b(r, m):
        b_ref[r:r + 1, ls] = m

    _top_values_sorted(_row_blocks(s1), PEER_TOPK, store_a)
    _top_values_sorted(_row_blocks(s2), PEER_TOPK, store_b)
    rank2 = jnp.zeros(s2.shape, F32)
    for r in range(PEER_TOPK):
        rank2 = jnp.where(s2 < b_ref[r:r + 1, ls], float(r + 1), rank2)
    a = a_ref[:, ls]
    off = 0
    for qq, n in _staircase():
        cand_ref[off:off + n, ls] = a[0:n, :] + b_ref[qq:qq + 1, ls]
        off += n
    if N_CAND_PAD > N_CAND:
        cand_ref[N_CAND:N_CAND_PAD, ls] = jnp.full((N_CAND_PAD - N_CAND, a.shape[1]), -jnp.inf, F32)
    cand = cand_ref[:, ls]
    tau = _top_values_sorted(_row_blocks(cand), PEER_TOPK, lambda r, m: None)
    z = jnp.sum(jnp.where(cand >= tau, jnp.exp(cand - cand[0:1, :]), 0.0), axis=0, keepdims=True)
    n1 = jnp.zeros(s1.shape, F32)
    for qq in range(PEER_TOPK):
        ok = a + b_ref[qq:qq + 1, ls] >= tau
        theta = jnp.min(jnp.where(ok, a, jnp.inf), axis=0, keepdims=True)
        n1 = jnp.where(s1 >= theta, float(qq + 1), n1)
    g1 = jnp.exp(s1 - a[0:1, :])
    g2 = jnp.exp(s2 - b_ref[0:1, ls]) * (1.0 / z)
    return rank2, n1, g1, g2


def _bf16_bits(x, small_int=False):
    return pltpu.bitcast(x if small_int else x.astype(BF16).astype(F32), jnp.uint32)


def _dup_bf16_words(x, small_int=False):
    bits = _bf16_bits(x, small_int)
    return bits | lax.shift_right_logical(bits, jnp.uint32(16))


def _store_row_pairs(dst, x, slab_ref, small_int=False):
    rows, cols = x.shape
    for c in range(cols // V7X_LANES):
        cs = slice(c * V7X_LANES, (c + 1) * V7X_LANES)
        slab_ref[c] = x[:, cs]
        even = slab_ref[c, pl.ds(0, rows // 2, stride=2), :]
        odd = slab_ref[c, pl.ds(1, rows // 2, stride=2), :]
        dst[:, cs] = (lax.shift_right_logical(_bf16_bits(even, small_int), jnp.uint32(16))
                      | _bf16_bits(odd, small_int))


def _route_kernel(x_ref, nw_ref, wqry_ref, keys_ref, hb_ref, r2_ref, n1_ref, g1_ref, g2_ref,
                  a_ref, b_ref, cand_ref, hslab_ref, kslab_ref, s1_ref, s2_ref):
    hf = _rms(x_ref[...], nw_ref[...])
    tm = hf.shape[0]
    _store_row_pairs(hb_ref, hf, hslab_ref)
    q = jnp.dot(hf.astype(BF16), wqry_ref[...], preferred_element_type=F32).astype(BF16)
    for hd in range(PEER_HEADS):
        c1 = (2 * hd) * PEER_DK
        c2 = (2 * hd + 1) * PEER_DK
        s1_ref[...] = lax.dot_general(keys_ref[2 * hd], q[:, c1:c1 + PEER_DK], NT_DIMS,
                                      preferred_element_type=F32)
        s2_ref[...] = lax.dot_general(keys_ref[2 * hd + 1], q[:, c2:c2 + PEER_DK], NT_DIMS,
                                      preferred_element_type=F32)
        for lt in range(tm // V7X_LANES):
            ls = slice(lt * V7X_LANES, (lt + 1) * V7X_LANES)
            rank2, n1, g1, g2 = _select(s1_ref[:, ls], s2_ref[:, ls], a_ref, b_ref, cand_ref, ls)
            _store_row_pairs(r2_ref.at[hd, :, ls], rank2, kslab_ref.at[pl.ds(2 * lt, 1)], small_int=True)
            n1_ref[hd, :, ls] = _dup_bf16_words(n1, small_int=True)
            g1_ref[hd, :, ls] = _dup_bf16_words(g1)
            _store_row_pairs(g2_ref.at[hd, :, ls], g2, kslab_ref.at[pl.ds(2 * lt + 1, 1)])


def _route(x2, ffn_norm_w, w_query, keys, *, tm):
    n, d = x2.shape
    nq = w_query.shape[1]
    tab = jax.ShapeDtypeStruct((PEER_HEADS, PEER_N_KEYS, n), jnp.uint32)
    tabh = jax.ShapeDtypeStruct((PEER_HEADS, PEER_N_KEYS // 2, n), jnp.uint32)
    tab_spec = pl.BlockSpec((PEER_HEADS, PEER_N_KEYS, tm), lambda i: (0, 0, i))
    tabh_spec = pl.BlockSpec((PEER_HEADS, PEER_N_KEYS // 2, tm), lambda i: (0, 0, i))
    return pl.pallas_call(
        _route_kernel,
        grid=(n // tm,),
        in_specs=[
            pl.BlockSpec((tm, d), lambda i: (i, 0)),
            pl.BlockSpec((1, d), lambda i: (0, 0)),
            pl.BlockSpec((d, nq), lambda i: (0, 0)),
            pl.BlockSpec(keys.shape, lambda i: (0, 0, 0)),
        ],
        out_specs=[pl.BlockSpec((tm // 2, d), lambda i: (i, 0)), tabh_spec, tab_spec, tab_spec, tabh_spec],
        out_shape=[jax.ShapeDtypeStruct((n // 2, d), jnp.uint32), tabh, tab, tab, tabh],
        scratch_shapes=[
            pltpu.VMEM((PEER_TOPK, tm), F32),
            pltpu.VMEM((PEER_TOPK, tm), F32),
            pltpu.VMEM((N_CAND_PAD, tm), F32),
            pltpu.VMEM((d // V7X_LANES, tm, V7X_LANES), F32),
            pltpu.VMEM((2 * tm // V7X_LANES, PEER_N_KEYS, V7X_LANES), F32),
            pltpu.VMEM((PEER_N_KEYS, tm), F32),
            pltpu.VMEM((PEER_N_KEYS, tm), F32),
        ],
        compiler_params=pltpu.CompilerParams(
            dimension_semantics=("arbitrary",), vmem_limit_bytes=V7X_VMEM_LIMIT),
        name="route",
    )(x2, ffn_norm_w, w_query, keys)


def _tables_kernel(down_ref, up_ref, u_ref, vt_ref, uslab_ref, vslab_ref):
    _store_row_pairs(u_ref, down_ref[...], uslab_ref)
    _store_row_pairs(vt_ref, up_ref[...].T, vslab_ref)


def _tables(peer_down, peer_up, *, te):
    ne, d = peer_down.shape
    return pl.pallas_call(
        _tables_kernel,
        grid=(ne // te,),
        in_specs=[pl.BlockSpec((te, d), lambda i: (i, 0)), pl.BlockSpec((te, d), lambda i: (i, 0))],
        out_specs=[pl.BlockSpec((te // 2, d), lambda i: (i, 0)), pl.BlockSpec((d // 2, te), lambda i: (0, i))],
        out_shape=[jax.ShapeDtypeStruct((ne // 2, d), jnp.uint32), jax.ShapeDtypeStruct((d // 2, ne), jnp.uint32)],
        scratch_shapes=[pltpu.VMEM((d // V7X_LANES, te, V7X_LANES), F32),
                        pltpu.VMEM((te // V7X_LANES, d, V7X_LANES), F32)],
        compiler_params=pltpu.CompilerParams(
            dimension_semantics=("arbitrary",), vmem_limit_bytes=V7X_VMEM_LIMIT),
        name="tables",
    )(peer_down, peer_up)


def _divmod_nonneg(t, n):
    if n & (n - 1) == 0:
        return lax.shift_right_logical(t, n.bit_length() - 1), t & (n - 1)
    return t // n, lax.rem(t, n)


def _peer_kernel(hb_ref, u_ref, vt_ref, r2_ref, n1_ref, g1_ref, g2_ref, x2_ref, fnw_ref, o_ref,
                 s0_ref, s1_ref, h0_ref, h1_ref, acc_ref, *, tt, te, n_etiles):
    k = pl.program_id(0)
    kc = k - 2
    ec = _divmod_nonneg(jnp.maximum(kc, 0), n_etiles)[1]
    nk = PEER_N_KEYS
    lw = 2 * V7X_LANES

    @pl.when(k == 0)
    def _():
        for ref in (s0_ref, s1_ref, h0_ref, h1_ref):
            ref[...] = jnp.zeros_like(ref)

    @pl.when((k == 0) | ((kc >= 0) & (ec == 0)))
    def _():
        acc_ref[...] = jnp.zeros_like(acc_ref)

    mc = 2 * nk
    d = acc_ref.shape[0]
    pieces_a = [(m, l) for l in range(tt // lw) for m in range(te // mc)]
    kc = 8 * nk
    pieces_c = [(m, l, kk) for l in range(tt // lw) for kk in range(te // kc) for m in range(d // mc)]
    nb_rows = nk // 2
    ii_group = 2
    blocks_b = [(ig, jh, l) for l in range(tt // lw) for ig in range(te // nk // ii_group)
                for jh in range(nk // nb_rows)]

    def words(ref, *idx):
        return pltpu.bitcast(ref[idx], BF16)

    def stages(sa_ref, sb_ref, hb_out_ref, hc_ref):
        def stage_a(m, l):
            rs, ls = slice(m * mc, (m + 1) * mc), slice(l * lw, (l + 1) * lw)
            u = words(u_ref, slice(m * mc // 2, (m + 1) * mc // 2), slice(None))
            hb = words(hb_ref, slice(l * lw // 2, (l + 1) * lw // 2), slice(None))
            sa_ref[rs, ls] = lax.dot_general(u, hb, NT_DIMS, preferred_element_type=F32)

        def stage_b(ig, jh, l):
            ls = slice(l * lw, (l + 1) * lw)
            js = slice(jh * nb_rows // 2, (jh + 1) * nb_rows // 2)
            iis = range(ig * ii_group, (ig + 1) * ii_group)
            w = [jnp.zeros((nb_rows, lw), BF16) for _ in iis]
            for hd in range(PEER_HEADS):
                g2 = words(g2_ref, hd, js, ls)
                r2 = words(r2_ref, hd, js, ls)
                for t, ii in enumerate(iis):
                    n1 = pltpu.bitcast(jnp.broadcast_to(n1_ref[hd, ii:ii + 1, ls], (nb_rows // 2, lw)), BF16)
                    g1 = pltpu.bitcast(jnp.broadcast_to(g1_ref[hd, ii:ii + 1, ls], (nb_rows // 2, lw)), BF16)
                    w[t] = w[t] + jnp.where(r2 < n1, g2 * g1, jnp.zeros_like(g2))
            for t, ii in enumerate(iis):
                rs = slice(ii * nk + jh * nb_rows, ii * nk + (jh + 1) * nb_rows)
                s = sb_ref[rs, ls]
                act = 0.5 * s * (1.0 + lax.erf(s * 0.7071067811865476))
                hb_out_ref[rs, ls] = w[t] * act.astype(BF16)

        def stage_c(m, l, kk):
            rs, ls, ks = slice(m * mc, (m + 1) * mc), slice(l * lw, (l + 1) * lw), slice(kk * kc, (kk + 1) * kc)
            vt = words(vt_ref, slice(m * mc // 2, (m + 1) * mc // 2), ks)
            acc_ref[rs, ls] += jnp.dot(vt, hc_ref[ks, ls], preferred_element_type=F32)

        nb = len(blocks_b) // len(pieces_a)
        for p in range(len(pieces_a)):
            for q in range(nb):
                stage_b(*blocks_b[p * nb + q])
            stage_a(*pieces_a[p])
            stage_c(*pieces_c[p])

    @pl.when(lax.rem(k, 2) == 0)
    def _():
        stages(s0_ref, s1_ref, h1_ref, h0_ref)

    @pl.when(lax.rem(k, 2) == 1)
    def _():
        stages(s1_ref, s0_ref, h0_ref, h1_ref)

    @pl.when((kc >= 0) & (ec == n_etiles - 1))
    def _():
        z = x2_ref[...] + acc_ref[...].T
        o_ref[...] = _rms(z, fnw_ref[...])


def _peer(hb, u, vt, r2, n1, g1, g2, x2, final_norm_w, *, tt, te):
    n, d = x2.shape
    ne = vt.shape[1]
    n_ttiles, n_etiles = n // tt, ne // te
    n_tiles = n_ttiles * n_etiles
    ni = te // PEER_N_KEYS
    assert ni % V7X_SUBLANES == 0, "the first-half key rows of one expert tile must be whole sublane groups"

    def tile(k, lag):
        return _divmod_nonneg(jnp.clip(k - lag, 0, n_tiles - 1), n_etiles)

    tab_spec = pl.BlockSpec((PEER_HEADS, PEER_N_KEYS // 2, tt), lambda k: (0, 0, tile(k, 1)[0]))
    row_spec = pl.BlockSpec((PEER_HEADS, ni, tt), lambda k: (0, tile(k, 1)[1], tile(k, 1)[0]))
    return pl.pallas_call(
        functools.partial(_peer_kernel, tt=tt, te=te, n_etiles=n_etiles),
        grid=(n_tiles + 2,),
        in_specs=[
            pl.BlockSpec((tt // 2, d), lambda k: (tile(k, 0)[0], 0)),
            pl.BlockSpec((te // 2, d), lambda k: (tile(k, 0)[1], 0)),
            pl.BlockSpec((d // 2, te), lambda k: (0, tile(k, 2)[1])),
            tab_spec, row_spec, row_spec, tab_spec,
            pl.BlockSpec((tt, d), lambda k: (tile(k, 2)[0], 0)),
            pl.BlockSpec((1, d), lambda k: (0, 0)),
        ],
        out_specs=pl.BlockSpec((tt, d), lambda k: (tile(k, 2)[0], 0)),
        out_shape=jax.ShapeDtypeStruct((n, d), F32),
        scratch_shapes=[
            pltpu.VMEM((te, tt), F32),
            pltpu.VMEM((te, tt), F32),
            pltpu.VMEM((te, tt), BF16),
            pltpu.VMEM((te, tt), BF16),
            pltpu.VMEM((d, tt), F32),
        ],
        compiler_params=pltpu.CompilerParams(
            dimension_semantics=("arbitrary",), vmem_limit_bytes=V7X_VMEM_LIMIT),
        name="peer",
    )(hb, u, vt, r2, n1, g1, g2, x2, final_norm_w)


def kernel(x, mem, mix_norm_w, w_in, conv_w, conv_norm_w, hgrn_lb_logits, hgrn_norm_w, w_out, xattn_norm_w,
           mem_norm_w, wq_mem, wkv_mem, wo_mem, ffn_norm_w, peer_w_query, peer_sub_keys, peer_down, peer_up,
           final_norm_w):
    b, t, d = x.shape
    depth = w_in.shape[0]
    assert depth == 1, "single-layer problem"
    assert t % TOKEN_TILE == 0 and d % V7X_LANES == 0 and peer_down.shape[1] % EXPERT_TILE == 0
    row = lambda v: v.reshape(1, -1)
    kv = _kv(mem, row(mem_norm_w[0]), wkv_mem[0].astype(BF16))
    x2 = _mix_attn(x, row(mix_norm_w[0]), w_in[0].astype(BF16), conv_w[0], row(conv_norm_w[0]), hgrn_lb_logits,
                   row(hgrn_norm_w[0]), w_out[0].astype(BF16), row(xattn_norm_w[0]), wq_mem[0].astype(BF16), kv,
                   wo_mem[0].astype(BF16), tm=TOKEN_TILE)
    x2 = x2.reshape(b * t, d)
    keys = peer_sub_keys[0].reshape(2 * PEER_HEADS, PEER_N_KEYS, PEER_DK).astype(BF16)
    hb, r2, n1, g1, g2 = _route(x2, row(ffn_norm_w[0]), peer_w_query[0].astype(BF16), keys, tm=TOKEN_TILE)
    u, vt = _tables(peer_down[0], peer_up[0], te=TABLE_TILE)
    out = _peer(hb, u, vt, r2, n1, g1, g2, x2, row(final_norm_w), tt=TOKEN_TILE, te=EXPERT_TILE)
    return out.reshape(b, t, d)
```
